```python
import jax, jax.numpy as jnp
from jax import lax
import numpy as np

D_MODEL = 1024
BATCH = 4
SEQ = 4096
DEPTH = 1

CHUNK = 64
PLE_DIM = 256
D_FF = 2816
MIX_WIDTH = D_MODEL
GLA_HEADS = 4
GLA_WIDTH = MIX_WIDTH // 2
GLA_DV = GLA_WIDTH // GLA_HEADS
GLA_DK = GLA_DV // 2
GLA_QK = GLA_HEADS * GLA_DK
GLA_GATE_RANK = 16
GLA_GATE_NORM = 16.0
RET_HEADS = 4
RET_WIDTH = MIX_WIDTH - GLA_WIDTH
RET_DV = RET_WIDTH // RET_HEADS
RET_DK = RET_DV // 2
RET_QK = RET_HEADS * RET_DK
ROPE_BASE = 10000.0
EPS = 1e-6
IN_COLS = 2 * GLA_QK + 2 * GLA_WIDTH + GLA_GATE_RANK + 2 * RET_QK + 2 * RET_WIDTH

kernel_name = "hymba_gla_retnet_macaron_ple"


def rms_norm(x, g):
    xf = x.astype(jnp.float32)
    y = xf * lax.rsqrt(jnp.mean(xf * xf, axis=-1, keepdims=True) + EPS)
    return (y * g.astype(jnp.float32)).astype(x.dtype)


def swiglu(x, w_gate, w_up, w_down):
    return (jax.nn.silu(x @ w_gate) * (x @ w_up)) @ w_down


def to_chunks(t, heads, dim):
    b, s, _ = t.shape
    return t.reshape(b, s // CHUNK, CHUNK, heads, dim).transpose(0, 3, 1, 2, 4)


def from_chunks(t):
    b, h, nc, c, d = t.shape
    return t.transpose(0, 2, 3, 1, 4).reshape(b, nc * c, h, d)


def chunk_state_scan(local_state, chunk_decay):
    def step(state, inp):
        u, a = inp
        return a * state + u, state
    xs = (jnp.moveaxis(local_state, 2, 0), jnp.moveaxis(chunk_decay, 2, 0))
    _, prev = lax.scan(step, jnp.zeros_like(local_state[:, :, 0]), xs)
    return jnp.moveaxis(prev, 0, 2)


def rope(t, heads, dim):
    b, s, _ = t.shape
    half = dim // 2
    inv = ROPE_BASE ** (-jnp.arange(half, dtype=jnp.float32) / half)
    ang = jnp.arange(s, dtype=jnp.float32)[:, None] * inv[None, :]
    cos = jnp.cos(ang)[None, :, None, :].astype(t.dtype)
    sin = jnp.sin(ang)[None, :, None, :].astype(t.dtype)
    th = t.reshape(b, s, heads, dim)
    t1, t2 = th[..., :half], th[..., half:]
    out = jnp.concatenate([t1 * cos - t2 * sin, t2 * cos + t1 * sin], axis=-1)
    return out.reshape(b, s, heads * dim)


def gla_mixer(q, k, v, r, gate_lr, w_gate_up, b_gate, g_norm):
    dt = q.dtype
    b_, s_, _ = q.shape
    logit = gate_lr @ w_gate_up + b_gate
    log_a = jax.nn.log_sigmoid(logit.astype(jnp.float32)) / GLA_GATE_NORM
    cum = jnp.cumsum(to_chunks(log_a, GLA_HEADS, GLA_DK), axis=3)
    cum_last = cum[:, :, :, -1:, :]
    qc = to_chunks(q, GLA_HEADS, GLA_DK) * (GLA_DK ** -0.5)
    kc = to_chunks(k, GLA_HEADS, GLA_DK)
    vc = to_chunks(v, GLA_HEADS, GLA_DV)
    e_pos = jnp.exp(cum).astype(dt)
    e_neg = jnp.exp(-cum).astype(dt)
    q_fwd = qc * e_pos
    idx = jnp.arange(CHUNK)
    causal = idx[:, None] >= idx[None, :]
    s_fwd = jnp.einsum('bhnid,bhnjd->bhnij', q_fwd, kc * e_neg)
    s_bwd = jnp.einsum('bhnid,bhnjd->bhnij', qc * e_neg, kc * e_pos)
    scores = jnp.where(causal, s_fwd, s_bwd)
    o_intra = jnp.einsum('bhnij,bhnjv->bhniv', scores, vc)
    k_state = kc * jnp.exp(cum_last - cum).astype(dt)
    local = jnp.einsum('bhnjd,bhnjv->bhndv', k_state, vc)
    decay = jnp.exp(cum_last[:, :, :, 0, :])[..., None].astype(dt)
    prev = chunk_state_scan(local, decay)
    o_inter = jnp.einsum('bhnid,bhndv->bhniv', q_fwd, prev)
    o = rms_norm(from_chunks(o_intra + o_inter), g_norm.reshape(GLA_HEADS, GLA_DV))
    return o.reshape(b_, s_, GLA_WIDTH) * jax.nn.silu(r)


def retention_mixer(q, k, v, g, g_norm):
    dt = q.dtype
    b_, s_, _ = q.shape
    log_gamma = jnp.log(1.0 - 2.0 ** (-5.0 - jnp.arange(RET_HEADS, dtype=jnp.float32)))
    qc = to_chunks(rope(q, RET_HEADS, RET_DK), RET_HEADS, RET_DK) * (RET_DK ** -0.5)
    kc = to_chunks(rope(k, RET_HEADS, RET_DK), RET_HEADS, RET_DK)
    vc = to_chunks(v, RET_HEADS, RET_DV)
    pos = jnp.arange(CHUNK, dtype=jnp.float32)
    dist = jnp.abs(pos[:, None] - pos[None, :])
    dmask = jnp.exp(log_gamma[:, None, None] * dist)[None, :, None].astype(dt)
    scores = jnp.einsum('bhnid,bhnjd->bhnij', qc, kc) * dmask
    o_intra = jnp.einsum('bhnij,bhnjv->bhniv', scores, vc)
    k_dec = jnp.exp(log_gamma[:, None] * (CHUNK - 1 - pos)[None, :])[None, :, None, :, None].astype(dt)
    local = jnp.einsum('bhnjd,bhnjv->bhndv', kc * k_dec, vc)
    decay = jnp.broadcast_to(jnp.exp(log_gamma * CHUNK)[None, :, None, None, None].astype(dt),
                             local.shape[:-1] + (1,))
    prev = chunk_state_scan(local, decay)
    q_dec = jnp.exp(log_gamma[:, None] * (pos + 1.0)[None, :])[None, :, None, :, None].astype(dt)
    o_inter = jnp.einsum('bhnid,bhndv->bhniv', qc * q_dec, prev)
    o = rms_norm(from_chunks(o_intra + o_inter), g_norm.reshape(RET_HEADS, RET_DV))
    return o.reshape(b_, s_, RET_WIDTH) * jax.nn.silu(g)


def setup_inputs(seed: int = 0) -> dict:
    key = jax.random.key(seed)
    ks = jax.random.split(key, 21)
    f32 = jnp.float32

    def w(k, shape, fan_in):
        return jax.random.normal(k, shape, f32) * (fan_in ** -0.5)

    def gain(k, shape):
        return 1.0 + 0.02 * jax.random.normal(k, shape, f32)

    L = DEPTH
    return {
        "x": jax.random.normal(ks[0], (BATCH, SEQ, D_MODEL), f32),
        "p": jax.random.normal(ks[1], (DEPTH, BATCH, SEQ, PLE_DIM), f32),
        "g_ffn1": gain(ks[2], (L, D_MODEL)),
        "w_ffn1_gate": w(ks[3], (L, D_MODEL, D_FF), D_MODEL),
        "w_ffn1_up": w(ks[4], (L, D_MODEL, D_FF), D_MODEL),
        "w_ffn1_down": w(ks[5], (L, D_FF, D_MODEL), D_FF),
        "g_mix": gain(ks[6], (L, D_MODEL)),
        "w_in": w(ks[7], (L, D_MODEL, IN_COLS), D_MODEL),
        "w_gla_gate_up": w(ks[8], (L, GLA_GATE_RANK, GLA_QK), GLA_GATE_RANK),
        "b_gla_gate": 0.1 * jax.random.normal(ks[9], (L, GLA_QK), f32),
        "g_gla_out": gain(ks[10], (L, GLA_WIDTH)),
        "g_ret_out": gain(ks[11], (L, RET_WIDTH)),
        "w_out": w(ks[12], (L, MIX_WIDTH, D_MODEL), MIX_WIDTH),
        "g_ffn2": gain(ks[13], (L, D_MODEL)),
        "w_ffn2_gate": w(ks[14], (L, D_MODEL, D_FF), D_MODEL),
        "w_ffn2_up": w(ks[15], (L, D_MODEL, D_FF), D_MODEL),
        "w_ffn2_down": w(ks[16], (L, D_FF, D_MODEL), D_FF),
        "g_ple": gain(ks[17], (L, D_MODEL)),
        "w_ple_gate": w(ks[18], (L, D_MODEL, D_MODEL), D_MODEL),
        "w_ple_proj": w(ks[19], (L, PLE_DIM, D_MODEL), PLE_DIM),
        "g_final": gain(ks[20], (D_MODEL,)),
    }


def reference(x, p, g_ffn1, w_ffn1_gate, w_ffn1_up, w_ffn1_down, g_mix, w_in, w_gla_gate_up,
              b_gla_gate, g_gla_out, g_ret_out, w_out, g_ffn2, w_ffn2_gate, w_ffn2_up, w_ffn2_down,
              g_ple, w_ple_gate, w_ple_proj, g_final):
    sizes = [GLA_QK, GLA_QK, GLA_WIDTH, GLA_WIDTH, GLA_GATE_RANK, RET_QK, RET_QK, RET_WIDTH, RET_WIDTH]
    offsets = [int(o) for o in np.cumsum(sizes)[:-1]]
    h = x
    for i in range(DEPTH):
        h = h + 0.5 * swiglu(rms_norm(h, g_ffn1[i]), w_ffn1_gate[i], w_ffn1_up[i], w_ffn1_down[i])
        z = rms_norm(h, g_mix[i]) @ w_in[i]
        gq, gk, gv, gr, glr, rq, rk, rv, rg = jnp.split(z, offsets, axis=-1)
        y_gla = gla_mixer(gq, gk, gv, gr, glr, w_gla_gate_up[i], b_gla_gate[i], g_gla_out[i])
        y_ret = retention_mixer(rq, rk, rv, rg, g_ret_out[i])
        h = h + jnp.concatenate([y_gla, y_ret], axis=-1) @ w_out[i]
        h = h + 0.5 * swiglu(rms_norm(h, g_ffn2[i]), w_ffn2_gate[i], w_ffn2_up[i], w_ffn2_down[i])
        gate = jax.nn.sigmoid(rms_norm(h, g_ple[i]) @ w_ple_gate[i])
        h = h + gate * (p[i] @ w_ple_proj[i])
    return rms_norm(h, g_final)
```

```python
import functools

import jax
import jax.numpy as jnp
import numpy as np
from jax import lax
from jax.experimental import pallas as pl
from jax.experimental.pallas import tpu as pltpu

CHUNK = 64
HEADS = 4
DK = 64
DV = 128
QK = HEADS * DK
VW = HEADS * DV
GATE_RANK = 16
GATE_RANK_PAD = 128
GATE_NORM = 16.0
ROPE_BASE = 10000.0
EPS = 1e-6
LANES = 128

TILE_A = 256
TILE_B = 256
VMEM_LIMIT = 56 * 1024 * 1024

F32 = jnp.float32
BF16 = jnp.bfloat16


def _dot(a, b):
    return jnp.dot(a, b, preferred_element_type=F32)


def _dot_nt(a, b):
    return lax.dot_general(a, b, (((1,), (1,)), ((), ())), preferred_element_type=F32)


def _dot_tn(a, b):
    return lax.dot_general(a, b, (((0,), (0,)), ((), ())), preferred_element_type=F32)


def _rms(x, g):
    return x * lax.rsqrt(jnp.mean(x * x, axis=-1, keepdims=True) + EPS) * g


def _sigmoid(x):
    return 1.0 / (1.0 + jnp.exp(-x))


def _silu(x):
    return x * _sigmoid(x)


def _log_sigmoid(x):
    return jnp.minimum(x, 0.0) - jnp.log1p(jnp.exp(-jnp.abs(x)))


def _swiglu(xn, wg_ref, wu_ref, wd_ref):
    g = _dot(xn, wg_ref[...])
    u = _dot(xn, wu_ref[...])
    return _dot((_silu(g) * u).astype(BF16), wd_ref[...])


def _tile4(a):
    return jnp.concatenate([a, a, a, a], axis=0)


def _call_a_kernel(x_ref, cos_ref, sin_ref, g1_ref, wg_ref, wu_ref, wd_ref, gmix_ref, win_ref,
                   wup_ref, bgate_ref, gout_ref, wout_ref, ll_ref,
                   mkg_ref, mkr_ref, mv_ref, mwg_ref, mwr_ref, causal_ref, dmask_ref,
                   qdec_ref, kdec_ref, rdecay_ref,
                   out_ref,
                   st_g, st_r, o_s, qf_s, qb_s, kf_s, kb_s, ks_s, gv_s, dec_s,
                   qr_s, qd_s, kr_s, kd_s, rv_s, *, tile):
    @pl.when(pl.program_id(1) == 0)
    def _():
        st_g[...] = jnp.zeros_like(st_g)
        st_r[...] = jnp.zeros_like(st_r)

    x = x_ref[0]
    f = _swiglu(_rms(x, g1_ref[...]).astype(BF16), wg_ref, wu_ref, wd_ref)
    h1 = x + 0.5 * f
    n = _rms(h1, gmix_ref[...]).astype(BF16)
    z = _dot(n, win_ref[...])
    o0 = 0
    gq = z[:, o0:o0 + QK]; o0 += QK
    gk = z[:, o0:o0 + QK]; o0 += QK
    gv = z[:, o0:o0 + VW]; o0 += VW
    gr = z[:, o0:o0 + VW]; o0 += VW
    rq = z[:, o0:o0 + QK]; o0 += QK
    rk = z[:, o0:o0 + QK]; o0 += QK
    rv = z[:, o0:o0 + VW]; o0 += VW
    rg = z[:, o0:o0 + VW]; o0 += VW
    glr = z[:, o0:o0 + GATE_RANK_PAD]

    logit = _dot(glr.astype(BF16), wup_ref[...]) + bgate_ref[...]
    log_a = _log_sigmoid(logit) * (1.0 / GATE_NORM)
    la_hi = log_a.astype(BF16)
    la_lo = (log_a - la_hi.astype(F32)).astype(BF16)
    ll = ll_ref[...]
    cc = _dot(ll, la_hi) + _dot(ll, la_lo)
    cum = cc[:tile]
    cl = cc[tile:]
    e_pos = jnp.exp(cum)
    e_neg = jnp.exp(-cum)
    qs = gq * (DK ** -0.5)
    qf_s[...] = (qs * e_pos).astype(BF16)
    qb_s[...] = (qs * e_neg).astype(BF16)
    kf_s[...] = (gk * e_neg).astype(BF16)
    kb_s[...] = (gk * e_pos).astype(BF16)
    ks_s[...] = (gk * jnp.exp(cl - cum)).astype(BF16)
    dec_s[...] = jnp.exp(cl)
    gv_s[...] = gv.astype(BF16)

    cos = cos_ref[...]
    sin = sin_ref[...]

    def rope(t):
        t1 = t[:, :LANES]
        t2 = t[:, LANES:]
        return jnp.concatenate([t1 * cos - t2 * sin, t2 * cos + t1 * sin], axis=1)

    qr = rope(rq) * (DK ** -0.5)
    kr = rope(rk)
    qr_s[...] = qr.astype(BF16)
    qd_s[...] = (qr * qdec_ref[...]).astype(BF16)
    kr_s[...] = kr.astype(BF16)
    kd_s[...] = (kr * kdec_ref[...]).astype(BF16)
    rv_s[...] = rv.astype(BF16)

    col = lax.broadcasted_iota(jnp.int32, (DV, QK), 1)
    head_g = col // DK
    head_r = (col % LANES) // (DK // 2)

    def state_update(st_ref, upd, head, decay_row):
        new = st_ref[...] * decay_row
        for h in range(HEADS):
            new = new + jnp.where(head == h, upd[h * DV:(h + 1) * DV, :], 0.0)
        st_ref[...] = new

    def chunk(c, carry):
        r0 = pl.multiple_of(c * CHUNK, CHUNK)
        rows = pl.ds(r0, CHUNK)
        mv = mv_ref[...]
        qf = qf_s[rows, :]
        s_f = _dot_nt(qf, _tile4(kf_s[rows, :]) * mkg_ref[...])
        s_b = _dot_nt(qb_s[rows, :], _tile4(kb_s[rows, :]) * mkg_ref[...])
        p = jnp.where(causal_ref[...] > 0.5, s_f, s_b).astype(BF16)
        v = gv_s[rows, :]
        w = _tile4(st_g[...].astype(BF16)) * mwg_ref[...]
        o_s[rows, 0:VW] = _dot(p, _tile4(v) * mv) + _dot_nt(qf, w)
        state_update(st_g, _dot_tn(v, ks_s[rows, :]), head_g, dec_s[pl.ds(r0, 1), :])
        s = _dot_nt(qr_s[rows, :], _tile4(kr_s[rows, :]) * mkr_ref[...]) * dmask_ref[...]
        v = rv_s[rows, :]
        w = _tile4(st_r[...].astype(BF16)) * mwr_ref[...]
        o_s[rows, VW:2 * VW] = _dot(s.astype(BF16), _tile4(v) * mv) + _dot_nt(qd_s[rows, :], w)
        state_update(st_r, _dot_tn(v, kd_s[rows, :]), head_r, rdecay_ref[...])
        return carry

    lax.fori_loop(0, tile // CHUNK, chunk, 0)

    gout = gout_ref[...]
    gates = (gr, rg)
    parts = []
    for h in range(2 * HEADS):
        oh = o_s[:, h * DV:(h + 1) * DV]
        gate = gates[h // HEADS][:, (h % HEADS) * DV:(h % HEADS + 1) * DV]
        parts.append((_rms(oh, gout[:, h * DV:(h + 1) * DV]) * _silu(gate)).astype(BF16))
    mix = jnp.concatenate(parts, axis=1)
    out_ref[0] = h1 + _dot(mix, wout_ref[...])


def _call_b_kernel(h_ref, p_ref, g2_ref, wg_ref, wu_ref, wd_ref, gple_ref, wpg_ref, wpp_ref, gfin_ref,
                   out_ref, *, final):
    h = h_ref[0]
    f = _swiglu(_rms(h, g2_ref[...]).astype(BF16), wg_ref, wu_ref, wd_ref)
    h3 = h + 0.5 * f
    gate = _sigmoid(_dot(_rms(h3, gple_ref[...]).astype(BF16), wpg_ref[...]))
    h4 = h3 + gate * _dot(p_ref[0].astype(BF16), wpp_ref[...])
    out_ref[0] = _rms(h4, gfin_ref[...]) if final else h4


def _const_spec(shape):
    return pl.BlockSpec(shape, lambda b, s: (0,) * len(shape), pipeline_mode=pl.Buffered(1))


def _mixer_constants(tile):
    r = np.arange(QK)
    kcol_g = r // DK
    kcol_r = (r % LANES) // (DK // 2)
    row_h = r // CHUNK
    mkg = (row_h[:, None] == kcol_g[None, :])
    mkr = (row_h[:, None] == kcol_r[None, :])
    mv = (row_h[:, None] == (np.arange(VW) // DV)[None, :])
    vrow_h = np.arange(VW) // DV
    mwg = (vrow_h[:, None] == kcol_g[None, :])
    mwr = (vrow_h[:, None] == kcol_r[None, :])
    i = np.arange(CHUNK)
    j = r % CHUNK
    causal = (i[:, None] >= j[None, :]).astype(np.float32)
    t = np.arange(tile)
    ll = np.concatenate([
        (t[:, None] // CHUNK == t[None, :] // CHUNK) & (t[None, :] <= t[:, None]),
        (t[:, None] // CHUNK == t[None, :] // CHUNK)], axis=0)

    log_gamma = jnp.log(1.0 - 2.0 ** (-5.0 - jnp.arange(HEADS, dtype=F32)))
    pos = jnp.arange(CHUNK, dtype=F32)
    dist = jnp.abs(pos[:, None] - pos[None, :])
    dmask = jnp.exp(log_gamma[:, None, None] * dist)
    dmask = dmask.transpose(1, 0, 2).reshape(CHUNK, QK)
    lg_col = log_gamma[kcol_r]
    qdec = jnp.exp(lg_col[None, :] * (pos + 1.0)[:, None])
    kdec = jnp.exp(lg_col[None, :] * (CHUNK - 1 - pos)[:, None])
    rdecay = jnp.exp(lg_col * CHUNK)[None, :]
    reps = tile // CHUNK
    as_bf = lambda m: jnp.asarray(m.astype(np.float32), dtype=BF16)
    return dict(ll=as_bf(ll), mkg=as_bf(mkg), mkr=as_bf(mkr), mv=as_bf(mv), mwg=as_bf(mwg), mwr=as_bf(mwr),
                causal=jnp.asarray(causal), dmask=dmask, qdec=jnp.tile(qdec, (reps, 1)),
                kdec=jnp.tile(kdec, (reps, 1)), rdecay=rdecay)


def _rope_tables(seq):
    half = DK // 2
    inv = ROPE_BASE ** (-jnp.arange(half, dtype=F32) / half)
    ang = jnp.arange(seq, dtype=F32)[:, None] * inv[None, :]
    return jnp.tile(jnp.cos(ang), (1, HEADS)), jnp.tile(jnp.sin(ang), (1, HEADS))


def _rope_column_order():
    half = DK // 2
    return np.array([h * DK + hf * half + i for hf in range(2) for h in range(HEADS) for i in range(half)])


def _call_a(h, cos, sin, consts, g1, wg, wu, wd, gmix, win, wup, bgate, gout, wout):
    b, s, d = h.shape
    tile = TILE_A
    ff = wg.shape[1]
    row = lambda a: a.reshape(1, -1)
    args = [h, cos, sin, row(g1), wg, wu, wd, row(gmix), win, wup, row(bgate), row(gout), wout,
            consts["ll"], consts["mkg"], consts["mkr"], consts["mv"], consts["mwg"], consts["mwr"],
            consts["causal"], consts["dmask"], consts["qdec"], consts["kdec"], consts["rdecay"]]
    in_specs = [pl.BlockSpec((1, tile, d), lambda bi, si: (bi, si, 0)),
                pl.BlockSpec((tile, LANES), lambda bi, si: (si, 0)),
                pl.BlockSpec((tile, LANES), lambda bi, si: (si, 0))]
    in_specs += [_const_spec(a.shape) for a in args[3:]]
    bf = lambda *shape: pltpu.VMEM(shape, BF16)
    scratch = [pltpu.VMEM((DV, QK), F32), pltpu.VMEM((DV, QK), F32), pltpu.VMEM((tile, 2 * VW), F32),
               bf(tile, QK), bf(tile, QK), bf(tile, QK), bf(tile, QK), bf(tile, QK), bf(tile, VW),
               pltpu.VMEM((tile, QK), F32),
               bf(tile, QK), bf(tile, QK), bf(tile, QK), bf(tile, QK), bf(tile, VW)]
    return pl.pallas_call(
        functools.partial(_call_a_kernel, tile=tile),
        grid=(b, s // tile),
        in_specs=in_specs,
        out_specs=pl.BlockSpec((1, tile, d), lambda bi, si: (bi, si, 0)),
        out_shape=jax.ShapeDtypeStruct((b, s, d), F32),
        scratch_shapes=scratch,
        compiler_params=pltpu.CompilerParams(dimension_semantics=("arbitrary", "arbitrary"),
                                             vmem_limit_bytes=VMEM_LIMIT),
        name="ffn_a_mixers",
    )(*args)


def _call_b(h, p, g2, wg, wu, wd, gple, wpg, wpp, gfin, final):
    b, s, d = h.shape
    tile = TILE_B
    row = lambda a: a.reshape(1, -1)
    args = [h, p, row(g2), wg, wu, wd, row(gple), wpg, wpp, row(gfin)]
    in_specs = [pl.BlockSpec((1, tile, d), lambda bi, si: (bi, si, 0)),
                pl.BlockSpec((1, tile, p.shape[-1]), lambda bi, si: (bi, si, 0))]
    in_specs += [_const_spec(a.shape) for a in args[2:]]
    return pl.pallas_call(
        functools.partial(_call_b_kernel, final=final),
        grid=(b, s // tile),
        in_specs=in_specs,
        out_specs=pl.BlockSpec((1, tile, d), lambda bi, si: (bi, si, 0)),
        out_shape=jax.ShapeDtypeStruct((b, s, d), F32),
        compiler_params=pltpu.CompilerParams(dimension_semantics=("arbitrary", "arbitrary"),
                                             vmem_limit_bytes=VMEM_LIMIT),
        name="ffn_b_embed",
    )(*args)


def kernel(x, p, g_ffn1, w_ffn1_gate, w_ffn1_up, w_ffn1_down, g_mix, w_in, w_gla_gate_up, b_gla_gate, g_gla_out, g_ret_out, w_out, g_ffn2, w_ffn2_gate, w_ffn2_up, w_ffn2_down, g_ple, w_ple_gate, w_ple_proj, g_final):
    depth = p.shape[0]
    seq = x.shape[1]
    assert seq % TILE_A == 0 and seq % TILE_B == 0 and TILE_A % CHUNK == 0
    consts = _mixer_constants(TILE_A)
    cos, sin = _rope_tables(seq)
    perm = _rope_column_order()
    sizes = [QK, QK, VW, VW, GATE_RANK, QK, QK, VW, VW]
    offs = np.concatenate([[0], np.cumsum(sizes)])
    bf = lambda a: a.astype(BF16)
    h = x
    for i in range(depth):
        wi = w_in[i]
        gq, gk, gv, gr, glr, rq, rk, rv, rg = [wi[:, offs[j]:offs[j + 1]] for j in range(9)]
        glr = jnp.pad(glr, ((0, 0), (0, GATE_RANK_PAD - GATE_RANK)))
        win = bf(jnp.concatenate([gq, gk, gv, gr, rq[:, perm], rk[:, perm], rv, rg, glr], axis=1))
        wup = bf(jnp.pad(w_gla_gate_up[i], ((0, GATE_RANK_PAD - GATE_RANK), (0, 0))))
        gout = jnp.concatenate([g_gla_out[i], g_ret_out[i]])
        h = _call_a(h, cos, sin, consts, g_ffn1[i], bf(w_ffn1_gate[i]), bf(w_ffn1_up[i]), bf(w_ffn1_down[i]),
                    g_mix[i], win, wup, b_gla_gate[i], gout, bf(w_out[i]))
        h = _call_b(h, p[i], g_ffn2[i], bf(w_ffn2_gate[i]), bf(w_ffn2_up[i]), bf(w_ffn2_down[i]),
                    g_ple[i], bf(w_ple_gate[i]), bf(w_ple_proj[i]), g_final, final=(i == depth - 1))
    return h
```

```python
import functools

import jax
import jax.numpy as jnp
import numpy as np
from jax import lax
from jax.experimental import pallas as pl
from jax.experimental.pallas import tpu as pltpu

CHUNK = 64
HEADS = 4
DK = 64
DV = 128
QK = HEADS * DK
VW = HEADS * DV
GATE_RANK = 16
GATE_RANK_PAD = 128
GATE_NORM = 16.0
ROPE_BASE = 10000.0
EPS = 1e-6
LANES = 128

TILE_A = 256
TILE_B = 512
VMEM_LIMIT = 56 * 1024 * 1024

F32 = jnp.float32
BF16 = jnp.bfloat16


def _dot(a, b):
    return jnp.dot(a, b, preferred_element_type=F32)


def _dot_nt(a, b):
    return lax.dot_general(a, b, (((1,), (1,)), ((), ())), preferred_element_type=F32)


def _dot_tn(a, b):
    return lax.dot_general(a, b, (((0,), (0,)), ((), ())), preferred_element_type=F32)


def _rms(x, g):
    return x * lax.rsqrt(jnp.mean(x * x, axis=-1, keepdims=True) + EPS) * g


def _sigmoid(x):
    return 1.0 / (1.0 + jnp.exp(-x))


def _silu(x):
    return x * _sigmoid(x)


def _log_sigmoid(x):
    return jnp.minimum(x, 0.0) - jnp.log1p(jnp.exp(-jnp.abs(x)))


def _swiglu(xn, wg_ref, wu_ref, wd_ref):
    g = _dot(xn, wg_ref[...])
    u = _dot(xn, wu_ref[...])
    return _dot((_silu(g) * u).astype(BF16), wd_ref[...])


def _tile4(a):
    return jnp.concatenate([a, a, a, a], axis=0)


def _call_a_kernel(x_ref, cos_ref, sin_ref, g1_ref, wg_ref, wu_ref, wd_ref, gmix_ref, win_ref,
                   wup_ref, bgate_ref, gout_ref, wout_ref, ll_ref,
                   mkg_ref, mkr_ref, mv_ref, mwg_ref, mwr_ref, causal_ref, dmask_ref,
                   qdec_ref, kdec_ref, rdecay_ref,
                   out_ref,
                   st_g, st_r, o_s, qf_s, qb_s, kf_s, kb_s, ks_s, gv_s, dec_s,
                   qr_s, qd_s, kr_s, kd_s, rv_s, *, tile):
    @pl.when(pl.program_id(1) == 0)
    def _():
        st_g[...] = jnp.zeros_like(st_g)
        st_r[...] = jnp.zeros_like(st_r)

    x = x_ref[0]
    f = _swiglu(_rms(x, g1_ref[...]).astype(BF16), wg_ref, wu_ref, wd_ref)
    h1 = x + 0.5 * f
    n = _rms(h1, gmix_ref[...]).astype(BF16)
    z = _dot(n, win_ref[...])
    o0 = 0
    gq = z[:, o0:o0 + QK]; o0 += QK
    gk = z[:, o0:o0 + QK]; o0 += QK
    gv = z[:, o0:o0 + VW]; o0 += VW
    gr = z[:, o0:o0 + VW]; o0 += VW
    rq = z[:, o0:o0 + QK]; o0 += QK
    rk = z[:, o0:o0 + QK]; o0 += QK
    rv = z[:, o0:o0 + VW]; o0 += VW
    rg = z[:, o0:o0 + VW]; o0 += VW
    glr = z[:, o0:o0 + GATE_RANK_PAD]

    logit = _dot(glr.astype(BF16), wup_ref[...]) + bgate_ref[...]
    log_a = _log_sigmoid(logit) * (1.0 / GATE_NORM)
    la_hi = log_a.astype(BF16)
    la_lo = (log_a - la_hi.astype(F32)).astype(BF16)
    ll = ll_ref[...]
    cc = _dot(ll, la_hi) + _dot(ll, la_lo)
    cum = cc[:tile]
    cl = cc[tile:]
    e_pos = jnp.exp(cum)
    e_neg = jnp.exp(-cum)
    qs = gq * (DK ** -0.5)
    qf_s[...] = (qs * e_pos).astype(BF16)
    qb_s[...] = (qs * e_neg).astype(BF16)
    kf_s[...] = (gk * e_neg).astype(BF16)
    kb_s[...] = (gk * e_pos).astype(BF16)
    ks_s[...] = (gk * jnp.exp(cl - cum)).astype(BF16)
    dec_s[...] = jnp.exp(cl)
    gv_s[...] = gv.astype(BF16)

    cos = cos_ref[...]
    sin = sin_ref[...]

    def rope(t):
        t1 = t[:, :LANES]
        t2 = t[:, LANES:]
        return jnp.concatenate([t1 * cos - t2 * sin, t2 * cos + t1 * sin], axis=1)

    qr = rope(rq) * (DK ** -0.5)
    kr = rope(rk)
    qr_s[...] = qr.astype(BF16)
    qd_s[...] = (qr * qdec_ref[...]).astype(BF16)
    kr_s[...] = kr.astype(BF16)
    kd_s[...] = (kr * kdec_ref[...]).astype(BF16)
    rv_s[...] = rv.astype(BF16)

    col = lax.broadcasted_iota(jnp.int32, (DV, QK), 1)
    head_g = col // DK
    head_r = (col % LANES) // (DK // 2)

    def state_update(st_ref, upd, head, decay_row):
        new = st_ref[...] * decay_row
        for h in range(HEADS):
            new = new + jnp.where(head == h, upd[h * DV:(h + 1) * DV, :], 0.0)
        st_ref[...] = new

    def chunk(c, carry):
        r0 = pl.multiple_of(c * CHUNK, CHUNK)
        rows = pl.ds(r0, CHUNK)
        mv = mv_ref[...]
        qf = qf_s[rows, :]
        s_f = _dot_nt(qf, _tile4(kf_s[rows, :]) * mkg_ref[...])
        s_b = _dot_nt(qb_s[rows, :], _tile4(kb_s[rows, :]) * mkg_ref[...])
        p = jnp.where(causal_ref[...] > 0.5, s_f, s_b).astype(BF16)
        v = gv_s[rows, :]
        w = _tile4(st_g[...].astype(BF16)) * mwg_ref[...]
        o_s[rows, 0:VW] = _dot(p, _tile4(v) * mv) + _dot_nt(qf, w)
        state_update(st_g, _dot_tn(v, ks_s[rows, :]), head_g, dec_s[pl.ds(r0, 1), :])
        s = _dot_nt(qr_s[rows, :], _tile4(kr_s[rows, :]) * mkr_ref[...]) * dmask_ref[...]
        v = rv_s[rows, :]
        w = _tile4(st_r[...].astype(BF16)) * mwr_ref[...]
        o_s[rows, VW:2 * VW] = _dot(s.astype(BF16), _tile4(v) * mv) + _dot_nt(qd_s[rows, :], w)
        state_update(st_r, _dot_tn(v, kd_s[rows, :]), head_r, rdecay_ref[...])
        return carry

    lax.fori_loop(0, tile // CHUNK, chunk, 0, unroll=True)

    gout = gout_ref[...]
    gates = (gr, rg)
    parts = []
    for h in range(2 * HEADS):
        oh = o_s[:, h * DV:(h + 1) * DV]
        gate = gates[h // HEADS][:, (h % HEADS) * DV:(h % HEADS + 1) * DV]
        parts.append((_rms(oh, gout[:, h * DV:(h + 1) * DV]) * _silu(gate)).astype(BF16))
    mix = jnp.concatenate(parts, axis=1)
    out_ref[0] = h1 + _dot(mix, wout_ref[...])


def _call_b_kernel(h_ref, p_ref, g2_ref, wg_ref, wu_ref, wd_ref, gple_ref, wpg_ref, wpp_ref, gfin_ref,
                   out_ref, *, final):
    h = h_ref[0]
    f = _swiglu(_rms(h, g2_ref[...]).astype(BF16), wg_ref, wu_ref, wd_ref)
    h3 = h + 0.5 * f
    gate = _sigmoid(_dot(_rms(h3, gple_ref[...]).astype(BF16), wpg_ref[...]))
    h4 = h3 + gate * _dot(p_ref[0].astype(BF16), wpp_ref[...])
    out_ref[0] = _rms(h4, gfin_ref[...]) if final else h4


def _const_spec(shape):
    return pl.BlockSpec(shape, lambda b, s: (0,) * len(shape), pipeline_mode=pl.Buffered(1))


def _mixer_constants(tile):
    r = np.arange(QK)
    kcol_g = r // DK
    kcol_r = (r % LANES) // (DK // 2)
    row_h = r // CHUNK
    mkg = (row_h[:, None] == kcol_g[None, :])
    mkr = (row_h[:, None] == kcol_r[None, :])
    mv = (row_h[:, None] == (np.arange(VW) // DV)[None, :])
    vrow_h = np.arange(VW) // DV
    mwg = (vrow_h[:, None] == kcol_g[None, :])
    mwr = (vrow_h[:, None] == kcol_r[None, :])
    i = np.arange(CHUNK)
    j = r % CHUNK
    causal = (i[:, None] >= j[None, :]).astype(np.float32)
    t = np.arange(tile)
    ll = np.concatenate([
        (t[:, None] // CHUNK == t[None, :] // CHUNK) & (t[None, :] <= t[:, None]),
        (t[:, None] // CHUNK == t[None, :] // CHUNK)], axis=0)

    log_gamma = jnp.log(1.0 - 2.0 ** (-5.0 - jnp.arange(HEADS, dtype=F32)))
    pos = jnp.arange(CHUNK, dtype=F32)
    dist = jnp.abs(pos[:, None] - pos[None, :])
    dmask = jnp.exp(log_gamma[:, None, None] * dist)
    dmask = dmask.transpose(1, 0, 2).reshape(CHUNK, QK)
    lg_col = log_gamma[kcol_r]
    qdec = jnp.exp(lg_col[None, :] * (pos + 1.0)[:, None])
    kdec = jnp.exp(lg_col[None, :] * (CHUNK - 1 - pos)[:, None])
    rdecay = jnp.exp(lg_col * CHUNK)[None, :]
    reps = tile // CHUNK
    as_bf = lambda m: jnp.asarray(m.astype(np.float32), dtype=BF16)
    return dict(ll=as_bf(ll), mkg=as_bf(mkg), mkr=as_bf(mkr), mv=as_bf(mv), mwg=as_bf(mwg), mwr=as_bf(mwr),
                causal=jnp.asarray(causal), dmask=dmask, qdec=jnp.tile(qdec, (reps, 1)),
                kdec=jnp.tile(kdec, (reps, 1)), rdecay=rdecay)


def _rope_tables(seq):
    half = DK // 2
    inv = ROPE_BASE ** (-jnp.arange(half, dtype=F32) / half)
    ang = jnp.arange(seq, dtype=F32)[:, None] * inv[None, :]
    return jnp.tile(jnp.cos(ang), (1, HEADS)), jnp.tile(jnp.sin(ang), (1, HEADS))


def _rope_column_order():
    half = DK // 2
    return np.array([h * DK + hf * half + i for hf in range(2) for h in range(HEADS) for i in range(half)])


def _call_a(h, cos, sin, consts, g1, wg, wu, wd, gmix, win, wup, bgate, gout, wout):
    b, s, d = h.shape
    tile = TILE_A
    ff = wg.shape[1]
    row = lambda a: a.reshape(1, -1)
    args = [h, cos, sin, row(g1), wg, wu, wd, row(gmix), win, wup, row(bgate), row(gout), wout,
            consts["ll"], consts["mkg"], consts["mkr"], consts["mv"], consts["mwg"], consts["mwr"],
            consts["causal"], consts["dmask"], consts["qdec"], consts["kdec"], consts["rdecay"]]
    in_specs = [pl.BlockSpec((1, tile, d), lambda bi, si: (bi, si, 0)),
                pl.BlockSpec((tile, LANES), lambda bi, si: (si, 0)),
                pl.BlockSpec((tile, LANES), lambda bi, si: (si, 0))]
    in_specs += [_const_spec(a.shape) for a in args[3:]]
    bf = lambda *shape: pltpu.VMEM(shape, BF16)
    scratch = [pltpu.VMEM((DV, QK), F32), pltpu.VMEM((DV, QK), F32), pltpu.VMEM((tile, 2 * VW), F32),
               bf(tile, QK), bf(tile, QK), bf(tile, QK), bf(tile, QK), bf(tile, QK), bf(tile, VW),
               pltpu.VMEM((tile, QK), F32),
               bf(tile, QK), bf(tile, QK), bf(tile, QK), bf(tile, QK), bf(tile, VW)]
    return pl.pallas_call(
        functools.partial(_call_a_kernel, tile=tile),
        grid=(b, s // tile),
        in_specs=in_specs,
        out_specs=pl.BlockSpec((1, tile, d), lambda bi, si: (bi, si, 0)),
        out_shape=jax.ShapeDtypeStruct((b, s, d), F32),
        scratch_shapes=scratch,
        compiler_params=pltpu.CompilerParams(dimension_semantics=("arbitrary", "arbitrary"),
                                             vmem_limit_bytes=VMEM_LIMIT),
        name="ffn_a_mixers",
    )(*args)


def _call_b(h, p, g2, wg, wu, wd, gple, wpg, wpp, gfin, final):
    b, s, d = h.shape
    tile = TILE_B
    row = lambda a: a.reshape(1, -1)
    args = [h, p, row(g2), wg, wu, wd, row(gple), wpg, wpp, row(gfin)]
    in_specs = [pl.BlockSpec((1, tile, d), lambda bi, si: (bi, si, 0)),
                pl.BlockSpec((1, tile, p.shape[-1]), lambda bi, si: (bi, si, 0))]
    in_specs += [_const_spec(a.shape) for a in args[2:]]
    return pl.pallas_call(
        functools.partial(_call_b_kernel, final=final),
        grid=(b, s // tile),
        in_specs=in_specs,
        out_specs=pl.BlockSpec((1, tile, d), lambda bi, si: (bi, si, 0)),
        out_shape=jax.ShapeDtypeStruct((b, s, d), F32),
        compiler_params=pltpu.CompilerParams(dimension_semantics=("arbitrary", "arbitrary"),
                                             vmem_limit_bytes=VMEM_LIMIT),
        name="ffn_b_embed",
    )(*args)


def kernel(x, p, g_ffn1, w_ffn1_gate, w_ffn1_up, w_ffn1_down, g_mix, w_in, w_gla_gate_up, b_gla_gate, g_gla_out, g_ret_out, w_out, g_ffn2, w_ffn2_gate, w_ffn2_up, w_ffn2_down, g_ple, w_ple_gate, w_ple_proj, g_final):
    depth = p.shape[0]
    seq = x.shape[1]
    assert seq % TILE_A == 0 and seq % TILE_B == 0 and TILE_A % CHUNK == 0
    consts = _mixer_constants(TILE_A)
    cos, sin = _rope_tables(seq)
    perm = _rope_column_order()
    sizes = [QK, QK, VW, VW, GATE_RANK, QK, QK, VW, VW]
    offs = np.concatenate([[0], np.cumsum(sizes)])
    bf = lambda a: a.astype(BF16)
    h = x
    for i in range(depth):
        wi = w_in[i]
        gq, gk, gv, gr, glr, rq, rk, rv, rg = [wi[:, offs[j]:offs[j + 1]] for j in range(9)]
        glr = jnp.pad(glr, ((0, 0), (0, GATE_RANK_PAD - GATE_RANK)))
        win = bf(jnp.concatenate([gq, gk, gv, gr, rq[:, perm], rk[:, perm], rv, rg, glr], axis=1))
        wup = bf(jnp.pad(w_gla_gate_up[i], ((0, GATE_RANK_PAD - GATE_RANK), (0, 0))))
        gout = jnp.concatenate([g_gla_out[i], g_ret_out[i]])
        h = _call_a(h, cos, sin, consts, g_ffn1[i], bf(w_ffn1_gate[i]), bf(w_ffn1_up[i]), bf(w_ffn1_down[i]),
                    g_mix[i], win, wup, b_gla_gate[i], gout, bf(w_out[i]))
        h = _call_b(h, p[i], g_ffn2[i], bf(w_ffn2_gate[i]), bf(w_ffn2_up[i]), bf(w_ffn2_down[i]),
                    g_ple[i], bf(w_ple_gate[i]), bf(w_ple_proj[i]), g_final, final=(i == depth - 1))
    return h
```

```python
import functools

import jax
import jax.numpy as jnp
import numpy as np
from jax import lax
from jax.experimental import pallas as pl
from jax.experimental.pallas import tpu as pltpu

CHUNK = 64
HEADS = 4
DK = 64
DV = 128
QK = HEADS * DK
VW = HEADS * DV
GATE_RANK = 16
GATE_RANK_PAD = 128
GATE_NORM = 16.0
ROPE_BASE = 10000.0
EPS = 1e-6
LANES = 128
MXU_COLS = 256

TILE_A = 256
TILE_B = 512
VMEM_LIMIT = 56 * 1024 * 1024

ZA_COLS = GATE_RANK_PAD + 4 * QK
ZB_COLS = 4 * VW

F32 = jnp.float32
BF16 = jnp.bfloat16


def _dot(a, b):
    return jnp.dot(a, b, preferred_element_type=F32)


def _dot_nt(a, b):
    return lax.dot_general(a, b, (((1,), (1,)), ((), ())), preferred_element_type=F32)


def _dot_tn(a, b):
    return lax.dot_general(a, b, (((0,), (0,)), ((), ())), preferred_element_type=F32)


def _rms(x, g):
    return x * lax.rsqrt(jnp.mean(x * x, axis=-1, keepdims=True) + EPS) * g


def _sigmoid(x):
    return 1.0 / (1.0 + jnp.exp(-x))


def _silu(x):
    return x * _sigmoid(x)


def _log_sigmoid(x):
    return jnp.minimum(x, 0.0) - jnp.log1p(jnp.exp(-jnp.abs(x)))


def _swiglu(xn, wg_ref, wu_ref, wd_ref):
    g = _dot(xn, wg_ref[...])
    u = _dot(xn, wu_ref[...])
    return _dot((_silu(g) * u).astype(BF16), wd_ref[...])


def _swiglu_up(xn, wg_ref, wu_ref, j):
    cols = slice(j * MXU_COLS, (j + 1) * MXU_COLS)
    return _dot(xn, wg_ref[:, cols]), _dot(xn, wu_ref[:, cols])


def _swiglu_down(gu, wd_ref, j):
    g, u = gu
    return _dot((_silu(g) * u).astype(BF16), wd_ref[j * MXU_COLS:(j + 1) * MXU_COLS, :])


def _tile4(a):
    return jnp.concatenate([a, a, a, a], axis=0)


def _call_a_kernel(x_ref, cos_ref, sin_ref, g1_ref, wg_ref, wu_ref, wd_ref, gmix_ref, win_ref,
                   wup_ref, bgate_ref, gout_ref, wout_ref, ll_ref,
                   mkg_ref, mkr_ref, mv_ref, mwg_ref, mwr_ref, causal_ref, dmask_ref,
                   qdec_ref, kdec_ref, rdecay_ref,
                   out_ref,
                   st_g, st_r, o_s, h1_s, gr_s, rg_s, qf_s, qb_s, kf_s, kb_s, ks_s, gv_s, dec_s,
                   qr_s, qd_s, kr_s, kd_s, rv_s, *, tile, tiles_per_seq):
    step = pl.program_id(0)
    carried = (o_s, h1_s, gr_s, rg_s, qf_s, qb_s, kf_s, kb_s, ks_s, gv_s, dec_s, qr_s, qd_s, kr_s, kd_s, rv_s)

    @pl.when(step == 0)
    def _():
        for ref in (st_g, st_r) + carried:
            ref[...] = jnp.zeros_like(ref)

    keep = lax.rem(jnp.maximum(step - 1, 0), tiles_per_seq) != 0
    st_g[...] = jnp.where(keep, st_g[...], 0.0)
    st_r[...] = jnp.where(keep, st_r[...], 0.0)

    col = lax.broadcasted_iota(jnp.int32, (DV, QK), 1)
    head_g = col // DK
    head_r = (col % LANES) // (DK // 2)

    def state_update(st_ref, upd, head, decay_row):
        new = st_ref[...] * decay_row
        for h in range(HEADS):
            new = new + jnp.where(head == h, upd[h * DV:(h + 1) * DV, :], 0.0)
        st_ref[...] = new

    def gla_chunk(c):
        rows = pl.ds(c * CHUNK, CHUNK)
        qf = qf_s[rows, :]
        s_f = _dot_nt(qf, _tile4(kf_s[rows, :]) * mkg_ref[...])
        s_b = _dot_nt(qb_s[rows, :], _tile4(kb_s[rows, :]) * mkg_ref[...])
        p = jnp.where(causal_ref[...] > 0.5, s_f, s_b).astype(BF16)
        v = gv_s[rows, :]
        w = _tile4(st_g[...].astype(BF16)) * mwg_ref[...]
        o_s[rows, 0:VW] = _dot(p, _tile4(v) * mv_ref[...]) + _dot_nt(qf, w)
        state_update(st_g, _dot_tn(v, ks_s[rows, :]), head_g, dec_s[pl.ds(c * CHUNK, 1), :])

    def ret_chunk(c):
        rows = pl.ds(c * CHUNK, CHUNK)
        s = _dot_nt(qr_s[rows, :], _tile4(kr_s[rows, :]) * mkr_ref[...]) * dmask_ref[...]
        v = rv_s[rows, :]
        w = _tile4(st_r[...].astype(BF16)) * mwr_ref[...]
        o_s[rows, VW:2 * VW] = _dot(s.astype(BF16), _tile4(v) * mv_ref[...]) + _dot_nt(qd_s[rows, :], w)
        state_update(st_r, _dot_tn(v, kd_s[rows, :]), head_r, rdecay_ref[...])

    def mix_operand():
        gout = gout_ref[...]
        gates = (gr_s, rg_s)
        parts = []
        for h in range(2 * HEADS):
            oh = o_s[:, h * DV:(h + 1) * DV]
            gate = gates[h // HEADS][:, (h % HEADS) * DV:(h % HEADS + 1) * DV]
            parts.append((_rms(oh, gout[:, h * DV:(h + 1) * DV]) * _silu(gate)).astype(BF16))
        return jnp.concatenate(parts, axis=1)

    mixer_units = []
    for c in range(tile // CHUNK):
        mixer_units += [functools.partial(gla_chunk, c), functools.partial(ret_chunk, c)]

    mixer_units[0]()
    x = x_ref[0]
    xn = _rms(x, g1_ref[...]).astype(BF16)
    n_pieces = wg_ref.shape[1] // MXU_COLS
    assert n_pieces >= len(mixer_units)
    f = None
    mix = None
    gu = _swiglu_up(xn, wg_ref, wu_ref, 0)
    for j in range(n_pieces):
        gu_next = _swiglu_up(xn, wg_ref, wu_ref, j + 1) if j + 1 < n_pieces else None
        fj = _swiglu_down(gu, wd_ref, j)
        f = fj if f is None else f + fj
        gu = gu_next
        if j + 1 < len(mixer_units):
            mixer_units[j + 1]()
        elif j + 1 == len(mixer_units):
            mix = mix_operand()
    h1 = x + 0.5 * f
    n = _rms(h1, gmix_ref[...]).astype(BF16)

    za = _dot(n, win_ref[:, 0:ZA_COLS])
    out_ref[0] = h1_s[...] + _dot(mix, wout_ref[...])
    glr = za[:, 0:GATE_RANK_PAD]
    o0 = GATE_RANK_PAD
    gq = za[:, o0:o0 + QK]; o0 += QK
    gk = za[:, o0:o0 + QK]; o0 += QK
    rq = za[:, o0:o0 + QK]; o0 += QK
    rk = za[:, o0:o0 + QK]

    logit = _dot(glr.astype(BF16), wup_ref[...]) + bgate_ref[...]
    log_a = _log_sigmoid(logit) * (1.0 / GATE_NORM)
    la_hi = log_a.astype(BF16)
    la_lo = (log_a - la_hi.astype(F32)).astype(BF16)
    ll = ll_ref[...]
    cc = _dot(ll, la_hi) + _dot(ll, la_lo)
    cum = cc[:tile]
    cl = cc[tile:]
    e_pos = jnp.exp(cum)
    e_neg = jnp.exp(-cum)
    qs = gq * (DK ** -0.5)

    cos = cos_ref[...]
    sin = sin_ref[...]

    def rope(t):
        t1 = t[:, :LANES]
        t2 = t[:, LANES:]
        return jnp.concatenate([t1 * cos - t2 * sin, t2 * cos + t1 * sin], axis=1)

    qr = rope(rq) * (DK ** -0.5)
    kr = rope(rk)

    h1_s[...] = h1
    qf_s[...] = (qs * e_pos).astype(BF16)
    qb_s[...] = (qs * e_neg).astype(BF16)
    kf_s[...] = (gk * e_neg).astype(BF16)
    kb_s[...] = (gk * e_pos).astype(BF16)
    ks_s[...] = (gk * jnp.exp(cl - cum)).astype(BF16)
    dec_s[...] = jnp.exp(cl)
    qr_s[...] = qr.astype(BF16)
    qd_s[...] = (qr * qdec_ref[...]).astype(BF16)
    kr_s[...] = kr.astype(BF16)
    kd_s[...] = (kr * kdec_ref[...]).astype(BF16)

    zb = _dot(n, win_ref[:, ZA_COLS:ZA_COLS + ZB_COLS])
    gv_s[...] = zb[:, 0:VW].astype(BF16)
    gr_s[...] = zb[:, VW:2 * VW]
    rv_s[...] = zb[:, 2 * VW:3 * VW].astype(BF16)
    rg_s[...] = zb[:, 3 * VW:4 * VW]


def _call_b_kernel(h_ref, p_ref, g2_ref, wg_ref, wu_ref, wd_ref, gple_ref, wpg_ref, wpp_ref, gfin_ref,
                   out_ref, *, final):
    h = h_ref[0]
    f = _swiglu(_rms(h, g2_ref[...]).astype(BF16), wg_ref, wu_ref, wd_ref)
    h3 = h + 0.5 * f
    gate = _sigmoid(_dot(_rms(h3, gple_ref[...]).astype(BF16), wpg_ref[...]))
    h4 = h3 + gate * _dot(p_ref[0].astype(BF16), wpp_ref[...])
    out_ref[0] = _rms(h4, gfin_ref[...]) if final else h4


def _const_spec(shape):
    return pl.BlockSpec(shape, lambda *_: (0,) * len(shape), pipeline_mode=pl.Buffered(1))


def _mixer_constants(tile):
    r = np.arange(QK)
    kcol_g = r // DK
    kcol_r = (r % LANES) // (DK // 2)
    row_h = r // CHUNK
    mkg = (row_h[:, None] == kcol_g[None, :])
    mkr = (row_h[:, None] == kcol_r[None, :])
    mv = (row_h[:, None] == (np.arange(VW) // DV)[None, :])
    vrow_h = np.arange(VW) // DV
    mwg = (vrow_h[:, None] == kcol_g[None, :])
    mwr = (vrow_h[:, None] == kcol_r[None, :])
    i = np.arange(CHUNK)
    j = r % CHUNK
    causal = (i[:, None] >= j[None, :]).astype(np.float32)
    t = np.arange(tile)
    ll = np.concatenate([
        (t[:, None] // CHUNK == t[None, :] // CHUNK) & (t[None, :] <= t[:, None]),
        (t[:, None] // CHUNK == t[None, :] // CHUNK)], axis=0)

    log_gamma = jnp.log(1.0 - 2.0 ** (-5.0 - jnp.arange(HEADS, dtype=F32)))
    pos = jnp.arange(CHUNK, dtype=F32)
    dist = jnp.abs(pos[:, None] - pos[None, :])
    dmask = jnp.exp(log_gamma[:, None, None] * dist)
    dmask = dmask.transpose(1, 0, 2).reshape(CHUNK, QK)
    lg_col = log_gamma[kcol_r]
    qdec = jnp.exp(lg_col[None, :] * (pos + 1.0)[:, None])
    kdec = jnp.exp(lg_col[None, :] * (CHUNK - 1 - pos)[:, None])
    rdecay = jnp.exp(lg_col * CHUNK)[None, :]
    reps = tile // CHUNK
    as_bf = lambda m: jnp.asarray(m.astype(np.float32), dtype=BF16)
    return dict(ll=as_bf(ll), mkg=as_bf(mkg), mkr=as_bf(mkr), mv=as_bf(mv), mwg=as_bf(mwg), mwr=as_bf(mwr),
                causal=jnp.asarray(causal), dmask=dmask, qdec=jnp.tile(qdec, (reps, 1)),
                kdec=jnp.tile(kdec, (reps, 1)), rdecay=rdecay)


def _rope_tables(seq):
    half = DK // 2
    inv = ROPE_BASE ** (-jnp.arange(half, dtype=F32) / half)
    ang = jnp.arange(seq, dtype=F32)[:, None] * inv[None, :]
    return jnp.tile(jnp.cos(ang), (1, HEADS)), jnp.tile(jnp.sin(ang), (1, HEADS))


def _rope_column_order():
    half = DK // 2
    return np.array([h * DK + hf * half + i for hf in range(2) for h in range(HEADS) for i in range(half)])


def _call_a(h, cos, sin, consts, g1, wg, wu, wd, gmix, win, wup, bgate, gout, wout):
    b, s, d = h.shape
    tile = TILE_A
    tps = s // tile
    n_tiles = b * tps
    row = lambda a: a.reshape(1, -1)
    args = [h, cos, sin, row(g1), wg, wu, wd, row(gmix), win, wup, row(bgate), row(gout), wout,
            consts["ll"], consts["mkg"], consts["mkr"], consts["mv"], consts["mwg"], consts["mwr"],
            consts["causal"], consts["dmask"], consts["qdec"], consts["kdec"], consts["rdecay"]]
    proj_tile = lambda st: jnp.minimum(st, n_tiles - 1)
    mix_tile = lambda st: jnp.maximum(st - 1, 0)
    in_specs = [pl.BlockSpec((1, tile, d), lambda st: (proj_tile(st) // tps, proj_tile(st) % tps, 0)),
                pl.BlockSpec((tile, LANES), lambda st: (proj_tile(st) % tps, 0)),
                pl.BlockSpec((tile, LANES), lambda st: (proj_tile(st) % tps, 0))]
    in_specs += [_const_spec(a.shape) for a in args[3:]]
    bf = lambda *shape: pltpu.VMEM(shape, BF16)
    f32 = lambda *shape: pltpu.VMEM(shape, F32)
    scratch = [f32(DV, QK), f32(DV, QK), f32(tile, 2 * VW), f32(tile, d), f32(tile, VW), f32(tile, VW),
               bf(tile, QK), bf(tile, QK), bf(tile, QK), bf(tile, QK), bf(tile, QK), bf(tile, VW),
               f32(tile, QK),
               bf(tile, QK), bf(tile, QK), bf(tile, QK), bf(tile, QK), bf(tile, VW)]
    return pl.pallas_call(
        functools.partial(_call_a_kernel, tile=tile, tiles_per_seq=tps),
        grid=(n_tiles + 1,),
        in_specs=in_specs,
        out_specs=pl.BlockSpec((1, tile, d), lambda st: (mix_tile(st) // tps, mix_tile(st) % tps, 0)),
        out_shape=jax.ShapeDtypeStruct((b, s, d), F32),
        scratch_shapes=scratch,
        compiler_params=pltpu.CompilerParams(dimension_semantics=("arbitrary",),
                                             vmem_limit_bytes=VMEM_LIMIT),
        name="ffn_a_mixers",
    )(*args)


def _call_b(h, p, g2, wg, wu, wd, gple, wpg, wpp, gfin, final):
    b, s, d = h.shape
    tile = TILE_B
    row = lambda a: a.reshape(1, -1)
    args = [h, p, row(g2), wg, wu, wd, row(gple), wpg, wpp, row(gfin)]
    in_specs = [pl.BlockSpec((1, tile, d), lambda bi, si: (bi, si, 0)),
                pl.BlockSpec((1, tile, p.shape[-1]), lambda bi, si: (bi, si, 0))]
    in_specs += [_const_spec(a.shape) for a in args[2:]]
    return pl.pallas_call(
        functools.partial(_call_b_kernel, final=final),
        grid=(b, s // tile),
        in_specs=in_specs,
        out_specs=pl.BlockSpec((1, tile, d), lambda bi, si: (bi, si, 0)),
        out_shape=jax.ShapeDtypeStruct((b, s, d), F32),
        compiler_params=pltpu.CompilerParams(dimension_semantics=("arbitrary", "arbitrary"),
                                             vmem_limit_bytes=VMEM_LIMIT),
        name="ffn_b_embed",
    )(*args)


def kernel(x, p, g_ffn1, w_ffn1_gate, w_ffn1_up, w_ffn1_down, g_mix, w_in, w_gla_gate_up, b_gla_gate, g_gla_out, g_ret_out, w_out, g_ffn2, w_ffn2_gate, w_ffn2_up, w_ffn2_down, g_ple, w_ple_gate, w_ple_proj, g_final):
    depth = p.shape[0]
    seq = x.shape[1]
    assert seq % TILE_A == 0 and seq % TILE_B == 0 and TILE_A % CHUNK == 0
    assert w_ffn1_gate.shape[2] % MXU_COLS == 0
    consts = _mixer_constants(TILE_A)
    cos, sin = _rope_tables(seq)
    perm = _rope_column_order()
    sizes = [QK, QK, VW, VW, GATE_RANK, QK, QK, VW, VW]
    offs = np.concatenate([[0], np.cumsum(sizes)])
    bf = lambda a: a.astype(BF16)
    h = x
    for i in range(depth):
        wi = w_in[i]
        gq, gk, gv, gr, glr, rq, rk, rv, rg = [wi[:, offs[j]:offs[j + 1]] for j in range(9)]
        glr = jnp.pad(glr, ((0, 0), (0, GATE_RANK_PAD - GATE_RANK)))
        win = bf(jnp.concatenate([glr, gq, gk, rq[:, perm], rk[:, perm], gv, gr, rv, rg], axis=1))
        wup = bf(jnp.pad(w_gla_gate_up[i], ((0, GATE_RANK_PAD - GATE_RANK), (0, 0))))
        gout = jnp.concatenate([g_gla_out[i], g_ret_out[i]])
        h = _call_a(h, cos, sin, consts, g_ffn1[i], bf(w_ffn1_gate[i]), bf(w_ffn1_up[i]), bf(w_ffn1_down[i]),
                    g_mix[i], win, wup, b_gla_gate[i], gout, bf(w_out[i]))
        h = _call_b(h, p[i], g_ffn2[i], bf(w_ffn2_gate[i]), bf(w_ffn2_up[i]), bf(w_ffn2_down[i]),
                    g_ple[i], bf(w_ple_gate[i]), bf(w_ple_proj[i]), g_final, final=(i == depth - 1))
    return h
```

```python
import functools

import jax
import jax.numpy as jnp
import numpy as np
from jax import lax
from jax.experimental import pallas as pl
from jax.experimental.pallas import tpu as pltpu

CHUNK = 64
HEADS = 4
DK = 64
DV = 128
QK = HEADS * DK
VW = HEADS * DV
GATE_RANK = 16
GATE_RANK_PAD = 128
GATE_NORM = 16.0
ROPE_BASE = 10000.0
EPS = 1e-6
LANES = 128
MXU_COLS = 256

TILE_A = 512
CUM_ROWS = 256
TILE_B = 512
VMEM_LIMIT = 56 * 1024 * 1024

ZA_COLS = GATE_RANK_PAD + 4 * QK
ZB_COLS = 4 * VW

F32 = jnp.float32
BF16 = jnp.bfloat16


def _dot(a, b):
    return jnp.dot(a, b, preferred_element_type=F32)


def _dot_nt(a, b):
    return lax.dot_general(a, b, (((1,), (1,)), ((), ())), preferred_element_type=F32)


def _dot_tn(a, b):
    return lax.dot_general(a, b, (((0,), (0,)), ((), ())), preferred_element_type=F32)


def _rms(x, g):
    return x * lax.rsqrt(jnp.mean(x * x, axis=-1, keepdims=True) + EPS) * g


def _sigmoid(x):
    return 1.0 / (1.0 + jnp.exp(-x))


def _silu(x):
    return x * _sigmoid(x)


def _log_sigmoid(x):
    return jnp.minimum(x, 0.0) - jnp.log1p(jnp.exp(-jnp.abs(x)))


def _swiglu(xn, wg_ref, wu_ref, wd_ref):
    g = _dot(xn, wg_ref[...])
    u = _dot(xn, wu_ref[...])
    return _dot((_silu(g) * u).astype(BF16), wd_ref[...])


def _swiglu_up(xn, wg_ref, wu_ref, j):
    cols = slice(j * MXU_COLS, (j + 1) * MXU_COLS)
    return _dot(xn, wg_ref[:, cols]), _dot(xn, wu_ref[:, cols])


def _swiglu_down(gu, wd_ref, j):
    g, u = gu
    return _dot((_silu(g) * u).astype(BF16), wd_ref[j * MXU_COLS:(j + 1) * MXU_COLS, :])


def _tile4(a):
    return jnp.concatenate([a, a, a, a], axis=0)


def _call_a_kernel(x_ref, cos_ref, sin_ref, g1_ref, wg_ref, wu_ref, wd_ref, gmix_ref, win_ref,
                   wup_ref, bgate_ref, gout_ref, wout_ref, ll_ref,
                   mkg_ref, mkr_ref, mv_ref, mwg_ref, mwr_ref, causal_ref, dmask_ref,
                   qdec_ref, kdec_ref, rdecay_ref,
                   out_ref,
                   st_g, st_r, o_s, h1_s, gr_s, rg_s, qf_s, qb_s, kf_s, kb_s, ks_s, gv_s, dec_s,
                   qr_s, qd_s, kr_s, kd_s, rv_s, *, tile, tiles_per_seq):
    step = pl.program_id(0)
    carried = (o_s, h1_s, gr_s, rg_s, qf_s, qb_s, kf_s, kb_s, ks_s, gv_s, dec_s, qr_s, qd_s, kr_s, kd_s, rv_s)

    @pl.when(step == 0)
    def _():
        for ref in (st_g, st_r) + carried:
            ref[...] = jnp.zeros_like(ref)

    keep = lax.rem(jnp.maximum(step - 1, 0), tiles_per_seq) != 0
    st_g[...] = jnp.where(keep, st_g[...], 0.0)
    st_r[...] = jnp.where(keep, st_r[...], 0.0)

    col = lax.broadcasted_iota(jnp.int32, (DV, QK), 1)
    head_g = col // DK
    head_r = (col % LANES) // (DK // 2)

    def state_update(st_ref, upd, head, decay_row):
        new = st_ref[...] * decay_row
        for h in range(HEADS):
            new = new + jnp.where(head == h, upd[h * DV:(h + 1) * DV, :], 0.0)
        st_ref[...] = new

    def gla_chunk(c):
        rows = pl.ds(c * CHUNK, CHUNK)
        qf = qf_s[rows, :]
        s_f = _dot_nt(qf, _tile4(kf_s[rows, :]) * mkg_ref[...])
        s_b = _dot_nt(qb_s[rows, :], _tile4(kb_s[rows, :]) * mkg_ref[...])
        p = jnp.where(causal_ref[...] > 0.5, s_f, s_b).astype(BF16)
        v = gv_s[rows, :]
        w = _tile4(st_g[...].astype(BF16)) * mwg_ref[...]
        o_s[rows, 0:VW] = _dot(p, _tile4(v) * mv_ref[...]) + _dot_nt(qf, w)
        state_update(st_g, _dot_tn(v, ks_s[rows, :]), head_g, dec_s[pl.ds(c * CHUNK, 1), :])

    def ret_chunk(c):
        rows = pl.ds(c * CHUNK, CHUNK)
        s = _dot_nt(qr_s[rows, :], _tile4(kr_s[rows, :]) * mkr_ref[...]) * dmask_ref[...]
        v = rv_s[rows, :]
        w = _tile4(st_r[...].astype(BF16)) * mwr_ref[...]
        o_s[rows, VW:2 * VW] = _dot(s.astype(BF16), _tile4(v) * mv_ref[...]) + _dot_nt(qd_s[rows, :], w)
        state_update(st_r, _dot_tn(v, kd_s[rows, :]), head_r, rdecay_ref[...])

    def mix_operand():
        gout = gout_ref[...]
        gates = (gr_s, rg_s)
        parts = []
        for h in range(2 * HEADS):
            oh = o_s[:, h * DV:(h + 1) * DV]
            gate = gates[h // HEADS][:, (h % HEADS) * DV:(h % HEADS + 1) * DV]
            parts.append((_rms(oh, gout[:, h * DV:(h + 1) * DV]) * _silu(gate)).astype(BF16))
        return jnp.concatenate(parts, axis=1)

    mixer_units = []
    for c in range(tile // CHUNK):
        mixer_units += [functools.partial(gla_chunk, c), functools.partial(ret_chunk, c)]

    mixer_units[0]()
    x = x_ref[0]
    xn = _rms(x, g1_ref[...]).astype(BF16)
    n_pieces = wg_ref.shape[1] // MXU_COLS
    spread = max(n_pieces - 2, 1)
    n_rest = len(mixer_units) - 1
    f = None
    mix = None
    gu = _swiglu_up(xn, wg_ref, wu_ref, 0)
    for j in range(n_pieces):
        gu_next = _swiglu_up(xn, wg_ref, wu_ref, j + 1) if j + 1 < n_pieces else None
        fj = _swiglu_down(gu, wd_ref, j)
        f = fj if f is None else f + fj
        gu = gu_next
        if j < spread:
            for u in range(1 + (j * n_rest) // spread, 1 + ((j + 1) * n_rest) // spread):
                mixer_units[u]()
        if j == spread - 1:
            mix = mix_operand()
    h1 = x + 0.5 * f
    n = _rms(h1, gmix_ref[...]).astype(BF16)

    za = _dot(n, win_ref[:, 0:ZA_COLS])
    out_ref[0] = h1_s[...] + _dot(mix, wout_ref[...])
    glr = za[:, 0:GATE_RANK_PAD]
    o0 = GATE_RANK_PAD
    gq = za[:, o0:o0 + QK]; o0 += QK
    gk = za[:, o0:o0 + QK]; o0 += QK
    rq = za[:, o0:o0 + QK]; o0 += QK
    rk = za[:, o0:o0 + QK]

    logit = _dot(glr.astype(BF16), wup_ref[...]) + bgate_ref[...]
    log_a = _log_sigmoid(logit) * (1.0 / GATE_NORM)
    la_hi = log_a.astype(BF16)
    la_lo = (log_a - la_hi.astype(F32)).astype(BF16)
    ll = ll_ref[...]
    cums, cls = [], []
    for r in range(tile // CUM_ROWS):
        blk = slice(r * CUM_ROWS, (r + 1) * CUM_ROWS)
        cc = _dot(ll, la_hi[blk]) + _dot(ll, la_lo[blk])
        cums.append(cc[:CUM_ROWS])
        cls.append(cc[CUM_ROWS:])
    cum = jnp.concatenate(cums, axis=0)
    cl = jnp.concatenate(cls, axis=0)
    e_pos = jnp.exp(cum)
    e_neg = jnp.exp(-cum)
    qs = gq * (DK ** -0.5)

    cos = cos_ref[...]
    sin = sin_ref[...]

    def rope(t):
        t1 = t[:, :LANES]
        t2 = t[:, LANES:]
        return jnp.concatenate([t1 * cos - t2 * sin, t2 * cos + t1 * sin], axis=1)

    qr = rope(rq) * (DK ** -0.5)
    kr = rope(rk)

    h1_s[...] = h1
    qf_s[...] = (qs * e_pos).astype(BF16)
    qb_s[...] = (qs * e_neg).astype(BF16)
    kf_s[...] = (gk * e_neg).astype(BF16)
    kb_s[...] = (gk * e_pos).astype(BF16)
    ks_s[...] = (gk * jnp.exp(cl - cum)).astype(BF16)
    dec_s[...] = jnp.exp(cl)
    qr_s[...] = qr.astype(BF16)
    qd_s[...] = (qr * qdec_ref[...]).astype(BF16)
    kr_s[...] = kr.astype(BF16)
    kd_s[...] = (kr * kdec_ref[...]).astype(BF16)

    zb = _dot(n, win_ref[:, ZA_COLS:ZA_COLS + ZB_COLS])
    gv_s[...] = zb[:, 0:VW].astype(BF16)
    gr_s[...] = zb[:, VW:2 * VW]
    rv_s[...] = zb[:, 2 * VW:3 * VW].astype(BF16)
    rg_s[...] = zb[:, 3 * VW:4 * VW]


def _call_b_kernel(h_ref, p_ref, g2_ref, wg_ref, wu_ref, wd_ref, gple_ref, wpg_ref, wpp_ref, gfin_ref,
                   out_ref, *, final):
    h = h_ref[0]
    f = _swiglu(_rms(h, g2_ref[...]).astype(BF16), wg_ref, wu_ref, wd_ref)
    h3 = h + 0.5 * f
    gate = _sigmoid(_dot(_rms(h3, gple_ref[...]).astype(BF16), wpg_ref[...]))
    h4 = h3 + gate * _dot(p_ref[0].astype(BF16), wpp_ref[...])
    out_ref[0] = _rms(h4, gfin_ref[...]) if final else h4


def _const_spec(shape):
    return pl.BlockSpec(shape, lambda *_: (0,) * len(shape), pipeline_mode=pl.Buffered(1))


def _mixer_constants(tile):
    r = np.arange(QK)
    kcol_g = r // DK
    kcol_r = (r % LANES) // (DK // 2)
    row_h = r // CHUNK
    mkg = (row_h[:, None] == kcol_g[None, :])
    mkr = (row_h[:, None] == kcol_r[None, :])
    mv = (row_h[:, None] == (np.arange(VW) // DV)[None, :])
    vrow_h = np.arange(VW) // DV
    mwg = (vrow_h[:, None] == kcol_g[None, :])
    mwr = (vrow_h[:, None] == kcol_r[None, :])
    i = np.arange(CHUNK)
    j = r % CHUNK
    causal = (i[:, None] >= j[None, :]).astype(np.float32)
    t = np.arange(CUM_ROWS)
    ll = np.concatenate([
        (t[:, None] // CHUNK == t[None, :] // CHUNK) & (t[None, :] <= t[:, None]),
        (t[:, None] // CHUNK == t[None, :] // CHUNK)], axis=0)

    log_gamma = jnp.log(1.0 - 2.0 ** (-5.0 - jnp.arange(HEADS, dtype=F32)))
    pos = jnp.arange(CHUNK, dtype=F32)
    dist = jnp.abs(pos[:, None] - pos[None, :])
    dmask = jnp.exp(log_gamma[:, None, None] * dist)
    dmask = dmask.transpose(1, 0, 2).reshape(CHUNK, QK)
    lg_col = log_gamma[kcol_r]
    qdec = jnp.exp(lg_col[None, :] * (pos + 1.0)[:, None])
    kdec = jnp.exp(lg_col[None, :] * (CHUNK - 1 - pos)[:, None])
    rdecay = jnp.exp(lg_col * CHUNK)[None, :]
    reps = tile // CHUNK
    as_bf = lambda m: jnp.asarray(m.astype(np.float32), dtype=BF16)
    return dict(ll=as_bf(ll), mkg=as_bf(mkg), mkr=as_bf(mkr), mv=as_bf(mv), mwg=as_bf(mwg), mwr=as_bf(mwr),
                causal=jnp.asarray(causal), dmask=dmask, qdec=jnp.tile(qdec, (reps, 1)),
                kdec=jnp.tile(kdec, (reps, 1)), rdecay=rdecay)


def _rope_tables(seq):
    half = DK // 2
    inv = ROPE_BASE ** (-jnp.arange(half, dtype=F32) / half)
    ang = jnp.arange(seq, dtype=F32)[:, None] * inv[None, :]
    return jnp.tile(jnp.cos(ang), (1, HEADS)), jnp.tile(jnp.sin(ang), (1, HEADS))


def _rope_column_order():
    half = DK // 2
    return np.array([h * DK + hf * half + i for hf in range(2) for h in range(HEADS) for i in range(half)])


def _call_a(h, cos, sin, consts, g1, wg, wu, wd, gmix, win, wup, bgate, gout, wout):
    b, s, d = h.shape
    tile = TILE_A
    tps = s // tile
    n_tiles = b * tps
    row = lambda a: a.reshape(1, -1)
    args = [h, cos, sin, row(g1), wg, wu, wd, row(gmix), win, wup, row(bgate), row(gout), wout,
            consts["ll"], consts["mkg"], consts["mkr"], consts["mv"], consts["mwg"], consts["mwr"],
            consts["causal"], consts["dmask"], consts["qdec"], consts["kdec"], consts["rdecay"]]
    proj_tile = lambda st: jnp.minimum(st, n_tiles - 1)
    mix_tile = lambda st: jnp.maximum(st - 1, 0)
    in_specs = [pl.BlockSpec((1, tile, d), lambda st: (proj_tile(st) // tps, proj_tile(st) % tps, 0)),
                pl.BlockSpec((tile, LANES), lambda st: (proj_tile(st) % tps, 0)),
                pl.BlockSpec((tile, LANES), lambda st: (proj_tile(st) % tps, 0))]
    in_specs += [_const_spec(a.shape) for a in args[3:]]
    bf = lambda *shape: pltpu.VMEM(shape, BF16)
    f32 = lambda *shape: pltpu.VMEM(shape, F32)
    scratch = [f32(DV, QK), f32(DV, QK), f32(tile, 2 * VW), f32(tile, d), f32(tile, VW), f32(tile, VW),
               bf(tile, QK), bf(tile, QK), bf(tile, QK), bf(tile, QK), bf(tile, QK), bf(tile, VW),
               f32(tile, QK),
               bf(tile, QK), bf(tile, QK), bf(tile, QK), bf(tile, QK), bf(tile, VW)]
    return pl.pallas_call(
        functools.partial(_call_a_kernel, tile=tile, tiles_per_seq=tps),
        grid=(n_tiles + 1,),
        in_specs=in_specs,
        out_specs=pl.BlockSpec((1, tile, d), lambda st: (mix_tile(st) // tps, mix_tile(st) % tps, 0)),
        out_shape=jax.ShapeDtypeStruct((b, s, d), F32),
        scratch_shapes=scratch,
        compiler_params=pltpu.CompilerParams(dimension_semantics=("arbitrary",),
                                             vmem_limit_bytes=VMEM_LIMIT),
        name="ffn_a_mixers",
    )(*args)


def _call_b(h, p, g2, wg, wu, wd, gple, wpg, wpp, gfin, final):
    b, s, d = h.shape
    tile = TILE_B
    row = lambda a: a.reshape(1, -1)
    args = [h, p, row(g2), wg, wu, wd, row(gple), wpg, wpp, row(gfin)]
    in_specs = [pl.BlockSpec((1, tile, d), lambda bi, si: (bi, si, 0)),
                pl.BlockSpec((1, tile, p.shape[-1]), lambda bi, si: (bi, si, 0))]
    in_specs += [_const_spec(a.shape) for a in args[2:]]
    return pl.pallas_call(
        functools.partial(_call_b_kernel, final=final),
        grid=(b, s // tile),
        in_specs=in_specs,
        out_specs=pl.BlockSpec((1, tile, d), lambda bi, si: (bi, si, 0)),
        out_shape=jax.ShapeDtypeStruct((b, s, d), F32),
        compiler_params=pltpu.CompilerParams(dimension_semantics=("arbitrary", "arbitrary"),
                                             vmem_limit_bytes=VMEM_LIMIT),
        name="ffn_b_embed",
    )(*args)


def kernel(x, p, g_ffn1, w_ffn1_gate, w_ffn1_up, w_ffn1_down, g_mix, w_in, w_gla_gate_up, b_gla_gate, g_gla_out, g_ret_out, w_out, g_ffn2, w_ffn2_gate, w_ffn2_up, w_ffn2_down, g_ple, w_ple_gate, w_ple_proj, g_final):
    depth = p.shape[0]
    seq = x.shape[1]
    assert seq % TILE_A == 0 and seq % TILE_B == 0 and TILE_A % CHUNK == 0
    assert w_ffn1_gate.shape[2] % MXU_COLS == 0
    consts = _mixer_constants(TILE_A)
    cos, sin = _rope_tables(seq)
    perm = _rope_column_order()
    sizes = [QK, QK, VW, VW, GATE_RANK, QK, QK, VW, VW]
    offs = np.concatenate([[0], np.cumsum(sizes)])
    bf = lambda a: a.astype(BF16)
    h = x
    for i in range(depth):
        wi = w_in[i]
        gq, gk, gv, gr, glr, rq, rk, rv, rg = [wi[:, offs[j]:offs[j + 1]] for j in range(9)]
        glr = jnp.pad(glr, ((0, 0), (0, GATE_RANK_PAD - GATE_RANK)))
        win = bf(jnp.concatenate([glr, gq, gk, rq[:, perm], rk[:, perm], gv, gr, rv, rg], axis=1))
        wup = bf(jnp.pad(w_gla_gate_up[i], ((0, GATE_RANK_PAD - GATE_RANK), (0, 0))))
        gout = jnp.concatenate([g_gla_out[i], g_ret_out[i]])
        h = _call_a(h, cos, sin, consts, g_ffn1[i], bf(w_ffn1_gate[i]), bf(w_ffn1_up[i]), bf(w_ffn1_down[i]),
                    g_mix[i], win, wup, b_gla_gate[i], gout, bf(w_out[i]))
        h = _call_b(h, p[i], g_ffn2[i], bf(w_ffn2_gate[i]), bf(w_ffn2_up[i]), bf(w_ffn2_down[i]),
                    g_ple[i], bf(w_ple_gate[i]), bf(w_ple_proj[i]), g_final, final=(i == depth - 1))
    return h
```

```python
import functools

import jax
import jax.numpy as jnp
import numpy as np
from jax import lax
from jax.experimental import pallas as pl
from jax.experimental.pallas import tpu as pltpu

CHUNK = 64
HEADS = 4
DK = 64
DV = 128
QK = HEADS * DK
VW = HEADS * DV
GATE_RANK = 16
GATE_RANK_PAD = 128
GATE_NORM = 16.0
ROPE_BASE = 10000.0
EPS = 1e-6
LANES = 128
MXU_COLS = 256

TILE_A = 512
CUM_ROWS = 256
TILE_B = 512
VMEM_LIMIT = 56 * 1024 * 1024

ZA_COLS = GATE_RANK_PAD + 4 * QK
ZB_COLS = 4 * VW

F32 = jnp.float32
BF16 = jnp.bfloat16


def _dot(a, b):
    return jnp.dot(a, b, preferred_element_type=F32)


def _dot_nt(a, b):
    return lax.dot_general(a, b, (((1,), (1,)), ((), ())), preferred_element_type=F32)


def _dot_tn(a, b):
    return lax.dot_general(a, b, (((0,), (0,)), ((), ())), preferred_element_type=F32)


def _rms(x, g):
    return x * lax.rsqrt(jnp.mean(x * x, axis=-1, keepdims=True) + EPS) * g


def _sigmoid(x):
    return 1.0 / (1.0 + jnp.exp(-x))


def _silu(x):
    return x * _sigmoid(x)


def _log_sigmoid(x):
    return jnp.minimum(x, 0.0) - jnp.log1p(jnp.exp(-jnp.abs(x)))


def _swiglu(xn, wg_ref, wu_ref, wd_ref):
    g = _dot(xn, wg_ref[...])
    u = _dot(xn, wu_ref[...])
    return _dot((_silu(g) * u).astype(BF16), wd_ref[...])


def _swiglu_up(xn, wgu_ref, j):
    gu = _dot(xn, wgu_ref[:, 2 * j * MXU_COLS:2 * (j + 1) * MXU_COLS])
    return gu[:, :MXU_COLS], gu[:, MXU_COLS:]


def _swiglu_down(gu, wd_ref, j):
    g, u = gu
    return _dot((_silu(g) * u).astype(BF16), wd_ref[j * MXU_COLS:(j + 1) * MXU_COLS, :])


def _tile4(a):
    return jnp.concatenate([a, a, a, a], axis=0)


def _lanes2(a):
    return jnp.concatenate([a, a], axis=1)


def _lanes4(a):
    return jnp.concatenate([a, a, a, a], axis=1)


def _store_chunks_transposed(dst_ref, k):
    for c in range(k.shape[0] // CHUNK):
        blk = k[c * CHUNK:(c + 1) * CHUNK, :]
        dst_ref[c * QK:(c + 1) * QK, :] = jnp.concatenate([blk, blk], axis=0).T.astype(BF16)


def _call_a_kernel(x_ref, cos_ref, sin_ref, g1_ref, wgu_ref, wd_ref, gmix_ref, win_ref,
                   wup_ref, bgate_ref, gout_ref, wout_ref, ll_ref,
                   mkg_ref, mkr_ref, mv_ref, mwg_ref, mwr_ref, causal_ref, dmask_ref,
                   qdec_ref, kdec_ref, rdecay_ref,
                   out_ref,
                   st_g, st_r, o_s, h1_s, gr_s, rg_s, qf_s, qb_s, kf_s, kb_s, ks_s, gv_s, dec_s,
                   qr_s, qd_s, kr_s, kd_s, rv_s, *, tile, tiles_per_seq):
    step = pl.program_id(0)
    carried = (o_s, h1_s, gr_s, rg_s, qf_s, qb_s, kf_s, kb_s, ks_s, gv_s, dec_s, qr_s, qd_s, kr_s, kd_s, rv_s)

    @pl.when(step == 0)
    def _():
        for ref in (st_g, st_r) + carried:
            ref[...] = jnp.zeros_like(ref)

    keep = lax.rem(jnp.maximum(step - 1, 0), tiles_per_seq) != 0
    st_g[...] = jnp.where(keep, st_g[...], 0.0)
    st_r[...] = jnp.where(keep, st_r[...], 0.0)

    col = lax.broadcasted_iota(jnp.int32, (DV, QK), 1)
    head_g = col // DK
    head_r = (col % LANES) // (DK // 2)

    def state_update(st_ref, upd, head, decay_row):
        new = st_ref[...] * decay_row
        for h in range(HEADS):
            new = new + jnp.where(head == h, upd[h * DV:(h + 1) * DV, :], 0.0)
        st_ref[...] = new

    def gla_chunk(c):
        rows = pl.ds(c * CHUNK, CHUNK)
        qf = qf_s[rows, :]
        krows = pl.ds(c * QK, QK)
        s_f = _dot(qf, _lanes2(kf_s[krows, :]) * mkg_ref[...])
        s_b = _dot(qb_s[rows, :], _lanes2(kb_s[krows, :]) * mkg_ref[...])
        p = jnp.where(causal_ref[...] > 0.5, s_f, s_b).astype(BF16)
        v = gv_s[rows, :]
        w = _lanes4(st_g[...].T.astype(BF16)) * mwg_ref[...]
        o_s[rows, 0:VW] = _dot(p, _tile4(v) * mv_ref[...]) + _dot(qf, w)
        state_update(st_g, _dot_tn(v, ks_s[rows, :]), head_g, dec_s[pl.ds(c * CHUNK, 1), :])

    def ret_chunk(c):
        rows = pl.ds(c * CHUNK, CHUNK)
        s = _dot(qr_s[rows, :], _lanes2(kr_s[pl.ds(c * QK, QK), :]) * mkr_ref[...]) * dmask_ref[...]
        v = rv_s[rows, :]
        w = _lanes4(st_r[...].T.astype(BF16)) * mwr_ref[...]
        o_s[rows, VW:2 * VW] = _dot(s.astype(BF16), _tile4(v) * mv_ref[...]) + _dot(qd_s[rows, :], w)
        state_update(st_r, _dot_tn(v, kd_s[rows, :]), head_r, rdecay_ref[...])

    def mix_operand():
        gout = gout_ref[...]
        gates = (gr_s, rg_s)
        parts = []
        for h in range(2 * HEADS):
            oh = o_s[:, h * DV:(h + 1) * DV]
            gate = gates[h // HEADS][:, (h % HEADS) * DV:(h % HEADS + 1) * DV]
            parts.append((_rms(oh, gout[:, h * DV:(h + 1) * DV]) * _silu(gate)).astype(BF16))
        return jnp.concatenate(parts, axis=1)

    mixer_units = []
    for c in range(tile // CHUNK):
        mixer_units += [functools.partial(gla_chunk, c), functools.partial(ret_chunk, c)]

    mixer_units[0]()
    x = x_ref[0]
    xn = _rms(x, g1_ref[...]).astype(BF16)
    n_pieces = wd_ref.shape[0] // MXU_COLS
    spread = max(n_pieces - 2, 1)
    n_rest = len(mixer_units) - 1
    f = None
    mix = None
    gu = _swiglu_up(xn, wgu_ref, 0)
    for j in range(n_pieces):
        gu_next = _swiglu_up(xn, wgu_ref, j + 1) if j + 1 < n_pieces else None
        fj = _swiglu_down(gu, wd_ref, j)
        f = fj if f is None else f + fj
        gu = gu_next
        if j < spread:
            for u in range(1 + (j * n_rest) // spread, 1 + ((j + 1) * n_rest) // spread):
                mixer_units[u]()
        if j == spread - 1:
            mix = mix_operand()
    h1 = x + 0.5 * f
    n = _rms(h1, gmix_ref[...]).astype(BF16)

    za = _dot(n, win_ref[:, 0:ZA_COLS])
    out_ref[0] = h1_s[...] + _dot(mix, wout_ref[...])
    glr = za[:, 0:GATE_RANK_PAD]
    o0 = GATE_RANK_PAD
    gq = za[:, o0:o0 + QK]; o0 += QK
    gk = za[:, o0:o0 + QK]; o0 += QK
    rq = za[:, o0:o0 + QK]; o0 += QK
    rk = za[:, o0:o0 + QK]

    logit = _dot(glr.astype(BF16), wup_ref[...]) + bgate_ref[...]
    log_a = _log_sigmoid(logit) * (1.0 / GATE_NORM)
    la_hi = log_a.astype(BF16)
    la_lo = (log_a - la_hi.astype(F32)).astype(BF16)
    ll = ll_ref[...]
    cums, cls = [], []
    for r in range(tile // CUM_ROWS):
        blk = slice(r * CUM_ROWS, (r + 1) * CUM_ROWS)
        cc = _dot(ll, la_hi[blk]) + _dot(ll, la_lo[blk])
        cums.append(cc[:CUM_ROWS])
        cls.append(cc[CUM_ROWS:])
    cum = jnp.concatenate(cums, axis=0)
    cl = jnp.concatenate(cls, axis=0)
    e_pos = jnp.exp(cum)
    e_neg = jnp.exp(-cum)
    qs = gq * (DK ** -0.5)

    cos = cos_ref[...]
    sin = sin_ref[...]

    def rope(t):
        t1 = t[:, :LANES]
        t2 = t[:, LANES:]
        return jnp.concatenate([t1 * cos - t2 * sin, t2 * cos + t1 * sin], axis=1)

    qr = rope(rq) * (DK ** -0.5)
    kr = rope(rk)

    h1_s[...] = h1
    qf_s[...] = (qs * e_pos).astype(BF16)
    qb_s[...] = (qs * e_neg).astype(BF16)
    _store_chunks_transposed(kf_s, gk * e_neg)
    _store_chunks_transposed(kb_s, gk * e_pos)
    ks_s[...] = (gk * jnp.exp(cl - cum)).astype(BF16)
    dec_s[...] = jnp.exp(cl)
    qr_s[...] = qr.astype(BF16)
    qd_s[...] = (qr * qdec_ref[...]).astype(BF16)
    _store_chunks_transposed(kr_s, kr)
    kd_s[...] = (kr * kdec_ref[...]).astype(BF16)

    zb = _dot(n, win_ref[:, ZA_COLS:ZA_COLS + ZB_COLS])
    gv_s[...] = zb[:, 0:VW].astype(BF16)
    gr_s[...] = zb[:, VW:2 * VW]
    rv_s[...] = zb[:, 2 * VW:3 * VW].astype(BF16)
    rg_s[...] = zb[:, 3 * VW:4 * VW]


def _call_b_kernel(h_ref, p_ref, g2_ref, wg_ref, wu_ref, wd_ref, gple_ref, wpg_ref, wpp_ref, gfin_ref,
                   out_ref, *, final):
    h = h_ref[0]
    f = _swiglu(_rms(h, g2_ref[...]).astype(BF16), wg_ref, wu_ref, wd_ref)
    h3 = h + 0.5 * f
    gate = _sigmoid(_dot(_rms(h3, gple_ref[...]).astype(BF16), wpg_ref[...]))
    h4 = h3 + gate * _dot(p_ref[0].astype(BF16), wpp_ref[...])
    out_ref[0] = _rms(h4, gfin_ref[...]) if final else h4


def _const_spec(shape):
    return pl.BlockSpec(shape, lambda *_: (0,) * len(shape), pipeline_mode=pl.Buffered(1))


def _mixer_constants(tile):
    r = np.arange(QK)
    kcol_g = r // DK
    kcol_r = (r % LANES) // (DK // 2)
    row_h = r // CHUNK
    mkg = (row_h[:, None] == kcol_g[None, :])
    mkr = (row_h[:, None] == kcol_r[None, :])
    mv = (row_h[:, None] == (np.arange(VW) // DV)[None, :])
    vrow_h = np.arange(VW) // DV
    mwg = (vrow_h[:, None] == kcol_g[None, :])
    mwr = (vrow_h[:, None] == kcol_r[None, :])
    i = np.arange(CHUNK)
    j = r % CHUNK
    causal = (i[:, None] >= j[None, :]).astype(np.float32)
    t = np.arange(CUM_ROWS)
    ll = np.concatenate([
        (t[:, None] // CHUNK == t[None, :] // CHUNK) & (t[None, :] <= t[:, None]),
        (t[:, None] // CHUNK == t[None, :] // CHUNK)], axis=0)

    log_gamma = np.log(1.0 - 2.0 ** (-5.0 - np.arange(HEADS, dtype=np.float64)))
    pos = np.arange(CHUNK, dtype=np.float64)
    dist = np.abs(pos[:, None] - pos[None, :])
    dmask = np.exp(log_gamma[:, None, None] * dist)
    dmask = dmask.transpose(1, 0, 2).reshape(CHUNK, QK)
    lg_col = log_gamma[kcol_r]
    qdec = np.exp(lg_col[None, :] * (pos + 1.0)[:, None])
    kdec = np.exp(lg_col[None, :] * (CHUNK - 1 - pos)[:, None])
    rdecay = np.exp(lg_col * CHUNK)[None, :]
    reps = tile // CHUNK
    as_bf = lambda m: jnp.asarray(m.astype(np.float32), dtype=BF16)
    as_f32 = lambda m: jnp.asarray(m.astype(np.float32))
    return dict(ll=as_bf(ll), mkg=as_bf(mkg.T), mkr=as_bf(mkr.T), mv=as_bf(mv), mwg=as_bf(mwg.T), mwr=as_bf(mwr.T),
                causal=as_f32(causal), dmask=as_f32(dmask), qdec=as_f32(np.tile(qdec, (reps, 1))),
                kdec=as_f32(np.tile(kdec, (reps, 1))), rdecay=as_f32(rdecay))


def _rope_tables(seq):
    half = DK // 2
    inv = ROPE_BASE ** (-np.arange(half, dtype=np.float64) / half)
    ang = np.arange(seq, dtype=np.float64)[:, None] * inv[None, :]
    table = lambda t: jnp.asarray(np.tile(t, (1, HEADS)).astype(np.float32))
    return table(np.cos(ang)), table(np.sin(ang))


def _rope_column_order():
    half = DK // 2
    return np.array([h * DK + hf * half + i for hf in range(2) for h in range(HEADS) for i in range(half)])


def _call_a(h, cos, sin, consts, g1, wgu, wd, gmix, win, wup, bgate, gout, wout):
    b, s, d = h.shape
    tile = TILE_A
    tps = s // tile
    n_tiles = b * tps
    row = lambda a: a.reshape(1, -1)
    args = [h, cos, sin, row(g1), wgu, wd, row(gmix), win, wup, row(bgate), row(gout), wout,
            consts["ll"], consts["mkg"], consts["mkr"], consts["mv"], consts["mwg"], consts["mwr"],
            consts["causal"], consts["dmask"], consts["qdec"], consts["kdec"], consts["rdecay"]]
    proj_tile = lambda st: jnp.minimum(st, n_tiles - 1)
    mix_tile = lambda st: jnp.maximum(st - 1, 0)
    in_specs = [pl.BlockSpec((1, tile, d), lambda st: (proj_tile(st) // tps, proj_tile(st) % tps, 0)),
                pl.BlockSpec((tile, LANES), lambda st: (proj_tile(st) % tps, 0)),
                pl.BlockSpec((tile, LANES), lambda st: (proj_tile(st) % tps, 0))]
    in_specs += [_const_spec(a.shape) for a in args[3:]]
    bf = lambda *shape: pltpu.VMEM(shape, BF16)
    f32 = lambda *shape: pltpu.VMEM(shape, F32)
    kt = bf(tile // CHUNK * QK, 2 * CHUNK)
    scratch = [f32(DV, QK), f32(DV, QK), f32(tile, 2 * VW), f32(tile, d), f32(tile, VW), f32(tile, VW),
               bf(tile, QK), bf(tile, QK), kt, kt, bf(tile, QK), bf(tile, VW),
               f32(tile, QK),
               bf(tile, QK), bf(tile, QK), kt, bf(tile, QK), bf(tile, VW)]
    return pl.pallas_call(
        functools.partial(_call_a_kernel, tile=tile, tiles_per_seq=tps),
        grid=(n_tiles + 1,),
        in_specs=in_specs,
        out_specs=pl.BlockSpec((1, tile, d), lambda st: (mix_tile(st) // tps, mix_tile(st) % tps, 0)),
        out_shape=jax.ShapeDtypeStruct((b, s, d), F32),
        scratch_shapes=scratch,
        compiler_params=pltpu.CompilerParams(dimension_semantics=("arbitrary",),
                                             vmem_limit_bytes=VMEM_LIMIT),
        name="ffn_a_mixers",
    )(*args)


def _call_b(h, p, g2, wg, wu, wd, gple, wpg, wpp, gfin, final):
    b, s, d = h.shape
    tile = TILE_B
    row = lambda a: a.reshape(1, -1)
    args = [h, p, row(g2), wg, wu, wd, row(gple), wpg, wpp, row(gfin)]
    in_specs = [pl.BlockSpec((1, tile, d), lambda bi, si: (bi, si, 0)),
                pl.BlockSpec((1, tile, p.shape[-1]), lambda bi, si: (bi, si, 0))]
    in_specs += [_const_spec(a.shape) for a in args[2:]]
    return pl.pallas_call(
        functools.partial(_call_b_kernel, final=final),
        grid=(b, s // tile),
        in_specs=in_specs,
        out_specs=pl.BlockSpec((1, tile, d), lambda bi, si: (bi, si, 0)),
        out_shape=jax.ShapeDtypeStruct((b, s, d), F32),
        compiler_params=pltpu.CompilerParams(dimension_semantics=("arbitrary", "arbitrary"),
                                             vmem_limit_bytes=VMEM_LIMIT),
        name="ffn_b_embed",
    )(*args)


def kernel(x, p, g_ffn1, w_ffn1_gate, w_ffn1_up, w_ffn1_down, g_mix, w_in, w_gla_gate_up, b_gla_gate, g_gla_out, g_ret_out, w_out, g_ffn2, w_ffn2_gate, w_ffn2_up, w_ffn2_down, g_ple, w_ple_gate, w_ple_proj, g_final):
    depth = p.shape[0]
    seq = x.shape[1]
    assert seq % TILE_A == 0 and seq % TILE_B == 0 and TILE_A % CHUNK == 0
    assert w_ffn1_gate.shape[2] % MXU_COLS == 0
    consts = _mixer_constants(TILE_A)
    cos, sin = _rope_tables(seq)
    perm = _rope_column_order()
    sizes = [QK, QK, VW, VW, GATE_RANK, QK, QK, VW, VW]
    offs = np.concatenate([[0], np.cumsum(sizes)])
    bf = lambda a: a.astype(BF16)
    h = x
    for i in range(depth):
        wi = bf(w_in[i])
        gq, gk, gv, gr, glr, rq, rk, rv, rg = [wi[:, offs[j]:offs[j + 1]] for j in range(9)]
        glr = jnp.pad(glr, ((0, 0), (0, GATE_RANK_PAD - GATE_RANK)))
        win = jnp.concatenate([glr, gq, gk, rq[:, perm], rk[:, perm], gv, gr, rv, rg], axis=1)
        wup = bf(jnp.pad(w_gla_gate_up[i], ((0, GATE_RANK_PAD - GATE_RANK), (0, 0))))
        gout = jnp.concatenate([g_gla_out[i], g_ret_out[i]])
        d_model, ff = w_ffn1_gate[i].shape
        pieces = lambda w: bf(w).reshape(d_model, ff // MXU_COLS, MXU_COLS)
        wgu = jnp.concatenate([pieces(w_ffn1_gate[i]), pieces(w_ffn1_up[i])], axis=2).reshape(d_model, 2 * ff)
        h = _call_a(h, cos, sin, consts, g_ffn1[i], wgu, bf(w_ffn1_down[i]),
                    g_mix[i], win, wup, b_gla_gate[i], gout, bf(w_out[i]))
        h = _call_b(h, p[i], g_ffn2[i], bf(w_ffn2_gate[i]), bf(w_ffn2_up[i]), bf(w_ffn2_down[i]),
                    g_ple[i], bf(w_ple_gate[i]), bf(w_ple_proj[i]), g_final, final=(i == depth - 1))
    return h
```

```python
import functools

import jax
import jax.numpy as jnp
import numpy as np
from jax import lax
from jax.experimental import pallas as pl
from jax.experimental.pallas import tpu as pltpu

CHUNK = 64
HEADS = 4
DK = 64
DV = 128
QK = HEADS * DK
VW = HEADS * DV
GATE_RANK = 16
GATE_RANK_PAD = 128
GATE_NORM = 16.0
ROPE_BASE = 10000.0
EPS = 1e-6
LANES = 128
MXU_COLS = 256

TILE_A = 512
CUM_ROWS = 256
TILE_B = 512
VMEM_LIMIT = 56 * 1024 * 1024

F32 = jnp.float32
BF16 = jnp.bfloat16


def _dot(a, b):
    return jnp.dot(a, b, preferred_element_type=F32)


def _dot_nt(a, b):
    return lax.dot_general(a, b, (((1,), (1,)), ((), ())), preferred_element_type=F32)


def _dot_tn(a, b):
    return lax.dot_general(a, b, (((0,), (0,)), ((), ())), preferred_element_type=F32)


def _rms(x, g):
    return x * lax.rsqrt(jnp.mean(x * x, axis=-1, keepdims=True) + EPS) * g


def _sigmoid(x):
    return 1.0 / (1.0 + jnp.exp(-x))


def _silu(x):
    return x * _sigmoid(x)


def _log_sigmoid(x):
    return jnp.minimum(x, 0.0) - jnp.log1p(jnp.exp(-jnp.abs(x)))


def _swiglu(xn, wg_ref, wu_ref, wd_ref):
    g = _dot(xn, wg_ref[...])
    u = _dot(xn, wu_ref[...])
    return _dot((_silu(g) * u).astype(BF16), wd_ref[...])


def _swiglu_up(xn, wg_ref, wu_ref, j):
    cols = slice(j * MXU_COLS, (j + 1) * MXU_COLS)
    return _dot(xn, wg_ref[:, cols]), _dot(xn, wu_ref[:, cols])


def _swiglu_down(gu, wd_ref, j):
    g, u = gu
    return _dot((_silu(g) * u).astype(BF16), wd_ref[j * MXU_COLS:(j + 1) * MXU_COLS, :])


def _tile4(a):
    return jnp.concatenate([a, a, a, a], axis=0)


def _lanes2(a):
    return jnp.concatenate([a, a], axis=1)


def _lanes4(a):
    return jnp.concatenate([a, a, a, a], axis=1)


def _store_chunks_transposed(dst_ref, k):
    for c in range(k.shape[0] // CHUNK):
        blk = k[c * CHUNK:(c + 1) * CHUNK, :]
        dst_ref[c * QK:(c + 1) * QK, :] = jnp.concatenate([blk, blk], axis=0).T.astype(BF16)


def _call_a_kernel(x_ref, cos_ref, sin_ref, g1_ref, wg_ref, wu_ref, wd_ref, gmix_ref,
                   wlr_ref, wgqk_ref, wrq_ref, wrk_ref, wgvr_ref, wrvg_ref,
                   wup_ref, bgate_ref, gout_ref, wout_ref, ll_ref,
                   mkg_ref, mkr_ref, mv_ref, mwg_ref, mwr_ref, causal_ref, dmask_ref,
                   qdec_ref, kdec_ref, rdecay_ref,
                   out_ref,
                   st_g, st_r, o_s, h1_s, gr_s, rg_s, qf_s, qb_s, kf_s, kb_s, ks_s, gv_s, dec_s,
                   qr_s, qd_s, kr_s, kd_s, rv_s, *, tile, tiles_per_seq):
    step = pl.program_id(0)
    carried = (o_s, h1_s, gr_s, rg_s, qf_s, qb_s, kf_s, kb_s, ks_s, gv_s, dec_s, qr_s, qd_s, kr_s, kd_s, rv_s)

    @pl.when(step == 0)
    def _():
        for ref in (st_g, st_r) + carried:
            ref[...] = jnp.zeros_like(ref)

    keep = lax.rem(jnp.maximum(step - 1, 0), tiles_per_seq) != 0
    st_g[...] = jnp.where(keep, st_g[...], 0.0)
    st_r[...] = jnp.where(keep, st_r[...], 0.0)

    col = lax.broadcasted_iota(jnp.int32, (DV, QK), 1)
    head_g = col // DK
    head_r = (col % LANES) // (DK // 2)

    def state_update(st_ref, upd, head, decay_row):
        new = st_ref[...] * decay_row
        for h in range(HEADS):
            new = new + jnp.where(head == h, upd[h * DV:(h + 1) * DV, :], 0.0)
        st_ref[...] = new

    def gla_chunk(c):
        rows = pl.ds(c * CHUNK, CHUNK)
        qf = qf_s[rows, :]
        krows = pl.ds(c * QK, QK)
        s_f = _dot(qf, _lanes2(kf_s[krows, :]) * mkg_ref[...])
        s_b = _dot(qb_s[rows, :], _lanes2(kb_s[krows, :]) * mkg_ref[...])
        p = jnp.where(causal_ref[...] > 0.5, s_f, s_b).astype(BF16)
        v = gv_s[rows, :]
        w = _lanes4(st_g[...].T.astype(BF16)) * mwg_ref[...]
        o_s[rows, 0:VW] = _dot(p, _tile4(v) * mv_ref[...]) + _dot(qf, w)
        state_update(st_g, _dot_tn(v, ks_s[rows, :]), head_g, dec_s[pl.ds(c * CHUNK, 1), :])

    def ret_chunk(c):
        rows = pl.ds(c * CHUNK, CHUNK)
        s = _dot(qr_s[rows, :], _lanes2(kr_s[pl.ds(c * QK, QK), :]) * mkr_ref[...]) * dmask_ref[...]
        v = rv_s[rows, :]
        w = _lanes4(st_r[...].T.astype(BF16)) * mwr_ref[...]
        o_s[rows, VW:2 * VW] = _dot(s.astype(BF16), _tile4(v) * mv_ref[...]) + _dot(qd_s[rows, :], w)
        state_update(st_r, _dot_tn(v, kd_s[rows, :]), head_r, rdecay_ref[...])

    def mix_operand():
        gout = gout_ref[...]
        gates = (gr_s, rg_s)
        parts = []
        for h in range(2 * HEADS):
            oh = o_s[:, h * DV:(h + 1) * DV]
            gate = gates[h // HEADS][:, (h % HEADS) * DV:(h % HEADS + 1) * DV]
            parts.append((_rms(oh, gout[:, h * DV:(h + 1) * DV]) * _silu(gate)).astype(BF16))
        return jnp.concatenate(parts, axis=1)

    mixer_units = []
    for c in range(tile // CHUNK):
        mixer_units += [functools.partial(gla_chunk, c), functools.partial(ret_chunk, c)]

    mixer_units[0]()
    x = x_ref[0]
    xn = _rms(x, g1_ref[...]).astype(BF16)
    n_pieces = wd_ref.shape[0] // MXU_COLS
    spread = max(n_pieces - 2, 1)
    n_rest = len(mixer_units) - 1
    f = None
    mix = None
    gu = _swiglu_up(xn, wg_ref, wu_ref, 0)
    for j in range(n_pieces):
        gu_next = _swiglu_up(xn, wg_ref, wu_ref, j + 1) if j + 1 < n_pieces else None
        fj = _swiglu_down(gu, wd_ref, j)
        f = fj if f is None else f + fj
        gu = gu_next
        if j < spread:
            for u in range(1 + (j * n_rest) // spread, 1 + ((j + 1) * n_rest) // spread):
                mixer_units[u]()
        if j == spread - 1:
            mix = mix_operand()
    h1 = x + 0.5 * f
    n = _rms(h1, gmix_ref[...]).astype(BF16)

    glr = _dot(n, wlr_ref[...])
    gqk = _dot(n, wgqk_ref[...])
    rq = _dot(n, wrq_ref[...])
    rk = _dot(n, wrk_ref[...])
    out_ref[0] = h1_s[...] + _dot(mix, wout_ref[...])
    gq = gqk[:, :QK]
    gk = gqk[:, QK:]

    logit = _dot(glr.astype(BF16), wup_ref[...]) + bgate_ref[...]
    log_a = _log_sigmoid(logit) * (1.0 / GATE_NORM)
    la_hi = log_a.astype(BF16)
    la_lo = (log_a - la_hi.astype(F32)).astype(BF16)
    ll = ll_ref[...]
    cums, cls = [], []
    for r in range(tile // CUM_ROWS):
        blk = slice(r * CUM_ROWS, (r + 1) * CUM_ROWS)
        cc = _dot(ll, la_hi[blk]) + _dot(ll, la_lo[blk])
        cums.append(cc[:CUM_ROWS])
        cls.append(cc[CUM_ROWS:])
    cum = jnp.concatenate(cums, axis=0)
    cl = jnp.concatenate(cls, axis=0)
    e_pos = jnp.exp(cum)
    e_neg = jnp.exp(-cum)
    qs = gq * (DK ** -0.5)

    cos = cos_ref[...]
    sin = sin_ref[...]

    def rope(t):
        t1 = t[:, :LANES]
        t2 = t[:, LANES:]
        return jnp.concatenate([t1 * cos - t2 * sin, t2 * cos + t1 * sin], axis=1)

    qr = rope(rq) * (DK ** -0.5)
    kr = rope(rk)

    h1_s[...] = h1
    qf_s[...] = (qs * e_pos).astype(BF16)
    qb_s[...] = (qs * e_neg).astype(BF16)
    _store_chunks_transposed(kf_s, gk * e_neg)
    _store_chunks_transposed(kb_s, gk * e_pos)
    ks_s[...] = (gk * jnp.exp(cl - cum)).astype(BF16)
    dec_s[...] = jnp.exp(cl)
    qr_s[...] = qr.astype(BF16)
    qd_s[...] = (qr * qdec_ref[...]).astype(BF16)
    _store_chunks_transposed(kr_s, kr)
    kd_s[...] = (kr * kdec_ref[...]).astype(BF16)

    gvr = _dot(n, wgvr_ref[...])
    gv_s[...] = gvr[:, :VW].astype(BF16)
    gr_s[...] = gvr[:, VW:]
    rvg = _dot(n, wrvg_ref[...])
    rv_s[...] = rvg[:, :VW].astype(BF16)
    rg_s[...] = rvg[:, VW:]


def _call_b_kernel(h_ref, p_ref, g2_ref, wg_ref, wu_ref, wd_ref, gple_ref, wpg_ref, wpp_ref, gfin_ref,
                   out_ref, *, final):
    h = h_ref[0]
    f = _swiglu(_rms(h, g2_ref[...]).astype(BF16), wg_ref, wu_ref, wd_ref)
    h3 = h + 0.5 * f
    gate = _sigmoid(_dot(_rms(h3, gple_ref[...]).astype(BF16), wpg_ref[...]))
    h4 = h3 + gate * _dot(p_ref[0].astype(BF16), wpp_ref[...])
    out_ref[0] = _rms(h4, gfin_ref[...]) if final else h4


def _const_spec(shape):
    return pl.BlockSpec(shape, lambda *_: (0,) * len(shape), pipeline_mode=pl.Buffered(1))


def _mixer_constants(tile):
    r = np.arange(QK)
    kcol_g = r // DK
    kcol_r = (r % LANES) // (DK // 2)
    row_h = r // CHUNK
    mkg = (row_h[:, None] == kcol_g[None, :])
    mkr = (row_h[:, None] == kcol_r[None, :])
    mv = (row_h[:, None] == (np.arange(VW) // DV)[None, :])
    vrow_h = np.arange(VW) // DV
    mwg = (vrow_h[:, None] == kcol_g[None, :])
    mwr = (vrow_h[:, None] == kcol_r[None, :])
    i = np.arange(CHUNK)
    j = r % CHUNK
    causal = (i[:, None] >= j[None, :]).astype(np.float32)
    t = np.arange(CUM_ROWS)
    ll = np.concatenate([
        (t[:, None] // CHUNK == t[None, :] // CHUNK) & (t[None, :] <= t[:, None]),
        (t[:, None] // CHUNK == t[None, :] // CHUNK)], axis=0)

    log_gamma = np.log(1.0 - 2.0 ** (-5.0 - np.arange(HEADS, dtype=np.float64)))
    pos = np.arange(CHUNK, dtype=np.float64)
    dist = np.abs(pos[:, None] - pos[None, :])
    dmask = np.exp(log_gamma[:, None, None] * dist)
    dmask = dmask.transpose(1, 0, 2).reshape(CHUNK, QK)
    lg_col = log_gamma[kcol_r]
    qdec = np.exp(lg_col[None, :] * (pos + 1.0)[:, None])
    kdec = np.exp(lg_col[None, :] * (CHUNK - 1 - pos)[:, None])
    rdecay = np.exp(lg_col * CHUNK)[None, :]
    reps = tile // CHUNK
    as_bf = lambda m: jnp.asarray(m.astype(np.float32), dtype=BF16)
    as_f32 = lambda m: jnp.asarray(m.astype(np.float32))
    return dict(ll=as_bf(ll), mkg=as_bf(mkg.T), mkr=as_bf(mkr.T), mv=as_bf(mv), mwg=as_bf(mwg.T), mwr=as_bf(mwr.T),
                causal=as_f32(causal), dmask=as_f32(dmask), qdec=as_f32(np.tile(qdec, (reps, 1))),
                kdec=as_f32(np.tile(kdec, (reps, 1))), rdecay=as_f32(rdecay))


def _rope_tables(seq):
    half = DK // 2
    inv = ROPE_BASE ** (-np.arange(half, dtype=np.float64) / half)
    ang = np.arange(seq, dtype=np.float64)[:, None] * inv[None, :]
    table = lambda t: jnp.asarray(np.tile(t, (1, HEADS)).astype(np.float32))
    return table(np.cos(ang)), table(np.sin(ang))


def _rope_halves_first(w):
    rows = w.shape[0]
    return w.reshape(rows, HEADS, 2, DK // 2).transpose(0, 2, 1, 3).reshape(rows, QK)


def _call_a(h, cos, sin, consts, g1, wg, wu, wd, gmix, win_parts, wup, bgate, gout, wout):
    b, s, d = h.shape
    tile = TILE_A
    tps = s // tile
    n_tiles = b * tps
    row = lambda a: a.reshape(1, -1)
    args = [h, cos, sin, row(g1), wg, wu, wd, row(gmix), *win_parts, wup, row(bgate), row(gout), wout,
            consts["ll"], consts["mkg"], consts["mkr"], consts["mv"], consts["mwg"], consts["mwr"],
            consts["causal"], consts["dmask"], consts["qdec"], consts["kdec"], consts["rdecay"]]
    proj_tile = lambda st: jnp.minimum(st, n_tiles - 1)
    mix_tile = lambda st: jnp.maximum(st - 1, 0)
    in_specs = [pl.BlockSpec((1, tile, d), lambda st: (proj_tile(st) // tps, proj_tile(st) % tps, 0)),
                pl.BlockSpec((tile, LANES), lambda st: (proj_tile(st) % tps, 0)),
                pl.BlockSpec((tile, LANES), lambda st: (proj_tile(st) % tps, 0))]
    in_specs += [_const_spec(a.shape) for a in args[3:]]
    bf = lambda *shape: pltpu.VMEM(shape, BF16)
    f32 = lambda *shape: pltpu.VMEM(shape, F32)
    kt = bf(tile // CHUNK * QK, 2 * CHUNK)
    scratch = [f32(DV, QK), f32(DV, QK), f32(tile, 2 * VW), f32(tile, d), f32(tile, VW), f32(tile, VW),
               bf(tile, QK), bf(tile, QK), kt, kt, bf(tile, QK), bf(tile, VW),
               f32(tile, QK),
               bf(tile, QK), bf(tile, QK), kt, bf(tile, QK), bf(tile, VW)]
    return pl.pallas_call(
        functools.partial(_call_a_kernel, tile=tile, tiles_per_seq=tps),
        grid=(n_tiles + 1,),
        in_specs=in_specs,
        out_specs=pl.BlockSpec((1, tile, d), lambda st: (mix_tile(st) // tps, mix_tile(st) % tps, 0)),
        out_shape=jax.ShapeDtypeStruct((b, s, d), F32),
        scratch_shapes=scratch,
        compiler_params=pltpu.CompilerParams(dimension_semantics=("arbitrary",),
                                             vmem_limit_bytes=VMEM_LIMIT),
        name="ffn_a_mixers",
    )(*args)


def _call_b(h, p, g2, wg, wu, wd, gple, wpg, wpp, gfin, final):
    b, s, d = h.shape
    tile = TILE_B
    row = lambda a: a.reshape(1, -1)
    args = [h, p, row(g2), wg, wu, wd, row(gple), wpg, wpp, row(gfin)]
    in_specs = [pl.BlockSpec((1, tile, d), lambda bi, si: (bi, si, 0)),
                pl.BlockSpec((1, tile, p.shape[-1]), lambda bi, si: (bi, si, 0))]
    in_specs += [_const_spec(a.shape) for a in args[2:]]
    return pl.pallas_call(
        functools.partial(_call_b_kernel, final=final),
        grid=(b, s // tile),
        in_specs=in_specs,
        out_specs=pl.BlockSpec((1, tile, d), lambda bi, si: (bi, si, 0)),
        out_shape=jax.ShapeDtypeStruct((b, s, d), F32),
        compiler_params=pltpu.CompilerParams(dimension_semantics=("arbitrary", "arbitrary"),
                                             vmem_limit_bytes=VMEM_LIMIT),
        name="ffn_b_embed",
    )(*args)


def kernel(x, p, g_ffn1, w_ffn1_gate, w_ffn1_up, w_ffn1_down, g_mix, w_in, w_gla_gate_up, b_gla_gate, g_gla_out, g_ret_out, w_out, g_ffn2, w_ffn2_gate, w_ffn2_up, w_ffn2_down, g_ple, w_ple_gate, w_ple_proj, g_final):
    depth = p.shape[0]
    seq = x.shape[1]
    assert seq % TILE_A == 0 and seq % TILE_B == 0 and TILE_A % CHUNK == 0
    assert w_ffn1_gate.shape[2] % MXU_COLS == 0
    consts = _mixer_constants(TILE_A)
    cos, sin = _rope_tables(seq)
    sizes = [QK, QK, VW, VW, GATE_RANK, QK, QK, VW, VW]
    offs = np.concatenate([[0], np.cumsum(sizes)])
    bf = lambda a: a.astype(BF16)
    h = x
    for i in range(depth):
        wi = w_in[i]
        cols = lambda j0, j1: bf(wi[:, offs[j0]:offs[j1]])
        win_parts = [jnp.pad(cols(4, 5), ((0, 0), (0, GATE_RANK_PAD - GATE_RANK))),
                     cols(0, 2),
                     _rope_halves_first(cols(5, 6)), _rope_halves_first(cols(6, 7)),
                     cols(2, 4), cols(7, 9)]
        wup = bf(jnp.pad(w_gla_gate_up[i], ((0, GATE_RANK_PAD - GATE_RANK), (0, 0))))
        gout = jnp.concatenate([g_gla_out[i], g_ret_out[i]])
        h = _call_a(h, cos, sin, consts, g_ffn1[i], bf(w_ffn1_gate[i]), bf(w_ffn1_up[i]), bf(w_ffn1_down[i]),
                    g_mix[i], win_parts, wup, b_gla_gate[i], gout, bf(w_out[i]))
        h = _call_b(h, p[i], g_ffn2[i], bf(w_ffn2_gate[i]), bf(w_ffn2_up[i]), bf(w_ffn2_down[i]),
                    g_ple[i], bf(w_ple_gate[i]), bf(w_ple_proj[i]), g_final, final=(i == depth - 1))
    return h
```

```python
import functools

import jax
import jax.numpy as jnp
import numpy as np
from jax import lax
from jax.experimental import pallas as pl
from jax.experimental.pallas import tpu as pltpu

CHUNK = 64
HEADS = 4
DK = 64
DV = 128
QK = HEADS * DK
VW = HEADS * DV
GATE_RANK = 16
GATE_RANK_PAD = 128
GATE_NORM = 16.0
ROPE_BASE = 10000.0
EPS = 1e-6
LANES = 128
MXU_COLS = 256

TILE_A = 512
CUM_ROWS = 256
TILE_B = 512
VMEM_LIMIT = 56 * 1024 * 1024

F32 = jnp.float32
BF16 = jnp.bfloat16


def _dot(a, b):
    return jnp.dot(a, b, preferred_element_type=F32)


def _dot_nt(a, b):
    return lax.dot_general(a, b, (((1,), (1,)), ((), ())), preferred_element_type=F32)


def _dot_tn(a, b):
    return lax.dot_general(a, b, (((0,), (0,)), ((), ())), preferred_element_type=F32)


def _rms(x, g):
    return x * lax.rsqrt(jnp.mean(x * x, axis=-1, keepdims=True) + EPS) * g


def _sigmoid(x):
    return 1.0 / (1.0 + jnp.exp(-x))


def _silu(x):
    return x * _sigmoid(x)


def _log_sigmoid(x):
    return jnp.minimum(x, 0.0) - jnp.log1p(jnp.exp(-jnp.abs(x)))


def _swiglu(xn, wg_ref, wu_ref, wd_ref):
    g = _dot(xn, wg_ref[...])
    u = _dot(xn, wu_ref[...])
    return _dot((_silu(g) * u).astype(BF16), wd_ref[...])


def _swiglu_up(xn, wg_ref, wu_ref, j):
    cols = slice(j * MXU_COLS, (j + 1) * MXU_COLS)
    return _dot(xn, wg_ref[:, cols]), _dot(xn, wu_ref[:, cols])


def _swiglu_down(gu, wd_ref, j):
    g, u = gu
    return _dot((_silu(g) * u).astype(BF16), wd_ref[j * MXU_COLS:(j + 1) * MXU_COLS, :])


def _tile4(a):
    return jnp.concatenate([a, a, a, a], axis=0)


def _lanes2(a):
    return jnp.concatenate([a, a], axis=1)


def _lanes4(a):
    return jnp.concatenate([a, a, a, a], axis=1)


def _store_chunks_transposed(dst_ref, k):
    for c in range(k.shape[0] // CHUNK):
        blk = k[c * CHUNK:(c + 1) * CHUNK, :]
        dst_ref[c * QK:(c + 1) * QK, :] = jnp.concatenate([blk, blk], axis=0).T.astype(dst_ref.dtype)


def _call_a_kernel(x_ref, cos_ref, sin_ref, g1_ref, wg_ref, wu_ref, wd_ref, gmix_ref,
                   wlr_ref, wgqk_ref, wrq_ref, wrk_ref, wgvr_ref, wrvg_ref,
                   wup_ref, bgate_ref, gout_ref, wout_ref, ll_ref,
                   mkg_ref, mkr_ref, mv_ref, mwg_ref, mwr_ref, causal_ref, dmask_ref,
                   qdec_ref, kdec_ref, rdecay_ref,
                   out_ref,
                   st_g, st_r, o_s, h1_s, gr_s, rg_s, qf_s, qb_s, kf_s, kb_s, ks_s, gv_s, dec_s,
                   qr_s, qd_s, kr_s, kd_s, rv_s, *, tile, tiles_per_seq):
    step = pl.program_id(0)
    carried = (o_s, h1_s, gr_s, rg_s, qf_s, qb_s, kf_s, kb_s, ks_s, gv_s, dec_s, qr_s, qd_s, kr_s, kd_s, rv_s)

    @pl.when(step == 0)
    def _():
        for ref in (st_g, st_r) + carried:
            ref[...] = jnp.zeros_like(ref)

    keep = lax.rem(jnp.maximum(step - 1, 0), tiles_per_seq) != 0
    st_g[...] = jnp.where(keep, st_g[...], 0.0)
    st_r[...] = jnp.where(keep, st_r[...], 0.0)

    half = DK // 2

    def gla_chunk(c):
        rows = pl.ds(c * CHUNK, CHUNK)
        qf = qf_s[rows, :]
        krows = pl.ds(c * QK, QK)
        s_f = _dot(qf, _lanes2(kf_s[krows, :]) * mkg_ref[...])
        s_b = _dot(qb_s[rows, :], _lanes2(kb_s[krows, :]) * mkg_ref[...])
        p = jnp.where(causal_ref[...] > 0.5, s_f, s_b).astype(BF16)
        v = gv_s[rows, :]
        st = st_g[...]
        w = _lanes4(st.astype(BF16)) * mwg_ref[...]
        o_s[rows, 0:VW] = _dot(p, _tile4(v) * mv_ref[...]) + _dot(qf, w)
        kt = ks_s[krows, :][:, :CHUNK]
        upd = [_dot(kt[h * DK:(h + 1) * DK, :], v[:, h * DV:(h + 1) * DV]) for h in range(HEADS)]
        st_g[...] = st * dec_s[krows, :] + jnp.concatenate(upd, axis=0)

    def ret_chunk(c):
        rows = pl.ds(c * CHUNK, CHUNK)
        s = _dot(qr_s[rows, :], _lanes2(kr_s[pl.ds(c * QK, QK), :]) * mkr_ref[...]) * dmask_ref[...]
        v = rv_s[rows, :]
        st = st_r[...]
        w = _lanes4(st.astype(BF16)) * mwr_ref[...]
        o_s[rows, VW:2 * VW] = _dot(s.astype(BF16), _tile4(v) * mv_ref[...]) + _dot(qd_s[rows, :], w)
        kt = kd_s[pl.ds(c * QK, QK), :][:, :CHUNK]
        upd = []
        for h in range(HEADS):
            k_h = jnp.concatenate([kt[h * half:(h + 1) * half, :],
                                   kt[LANES + h * half:LANES + (h + 1) * half, :]], axis=0)
            upd.append(_dot(k_h, v[:, h * DV:(h + 1) * DV]))
        upd = [u[:half] for u in upd] + [u[half:] for u in upd]
        st_r[...] = st * rdecay_ref[...] + jnp.concatenate(upd, axis=0)

    def mix_operand():
        gout = gout_ref[...]
        gates = (gr_s, rg_s)
        parts = []
        for h in range(2 * HEADS):
            oh = o_s[:, h * DV:(h + 1) * DV]
            gate = gates[h // HEADS][:, (h % HEADS) * DV:(h % HEADS + 1) * DV]
            parts.append((_rms(oh, gout[:, h * DV:(h + 1) * DV]) * _silu(gate)).astype(BF16))
        return jnp.concatenate(parts, axis=1)

    mixer_units = []
    for c in range(tile // CHUNK):
        mixer_units += [functools.partial(gla_chunk, c), functools.partial(ret_chunk, c)]

    mixer_units[0]()
    x = x_ref[0]
    xn = _rms(x, g1_ref[...]).astype(BF16)
    n_pieces = wd_ref.shape[0] // MXU_COLS
    spread = max(n_pieces - 2, 1)
    n_rest = len(mixer_units) - 1
    f = None
    mix = None
    gu = _swiglu_up(xn, wg_ref, wu_ref, 0)
    for j in range(n_pieces):
        gu_next = _swiglu_up(xn, wg_ref, wu_ref, j + 1) if j + 1 < n_pieces else None
        fj = _swiglu_down(gu, wd_ref, j)
        f = fj if f is None else f + fj
        gu = gu_next
        if j < spread:
            for u in range(1 + (j * n_rest) // spread, 1 + ((j + 1) * n_rest) // spread):
                mixer_units[u]()
        if j == spread - 1:
            mix = mix_operand()
    h1 = x + 0.5 * f
    n = _rms(h1, gmix_ref[...]).astype(BF16)

    glr = _dot(n, wlr_ref[...])
    gqk = _dot(n, wgqk_ref[...])
    rq = _dot(n, wrq_ref[...])
    rk = _dot(n, wrk_ref[...])
    out_ref[0] = h1_s[...] + _dot(mix, wout_ref[...])
    gq = gqk[:, :QK]
    gk = gqk[:, QK:]

    logit = _dot(glr.astype(BF16), wup_ref[...]) + bgate_ref[...]
    log_a = _log_sigmoid(logit) * (1.0 / GATE_NORM)
    la_hi = log_a.astype(BF16)
    la_lo = (log_a - la_hi.astype(F32)).astype(BF16)
    ll = ll_ref[...]
    cums, cls = [], []
    for r in range(tile // CUM_ROWS):
        blk = slice(r * CUM_ROWS, (r + 1) * CUM_ROWS)
        cc = _dot(ll, la_hi[blk]) + _dot(ll, la_lo[blk])
        cums.append(cc[:CUM_ROWS])
        cls.append(cc[CUM_ROWS:])
    cum = jnp.concatenate(cums, axis=0)
    cl = jnp.concatenate(cls, axis=0)
    e_pos = jnp.exp(cum)
    e_neg = jnp.exp(-cum)
    qs = gq * (DK ** -0.5)

    cos = cos_ref[...]
    sin = sin_ref[...]

    def rope(t):
        t1 = t[:, :LANES]
        t2 = t[:, LANES:]
        return jnp.concatenate([t1 * cos - t2 * sin, t2 * cos + t1 * sin], axis=1)

    qr = rope(rq) * (DK ** -0.5)
    kr = rope(rk)

    h1_s[...] = h1
    qf_s[...] = (qs * e_pos).astype(BF16)
    qb_s[...] = (qs * e_neg).astype(BF16)
    _store_chunks_transposed(kf_s, gk * e_neg)
    _store_chunks_transposed(kb_s, gk * e_pos)
    _store_chunks_transposed(ks_s, gk * jnp.exp(cl - cum))
    _store_chunks_transposed(dec_s, jnp.exp(cl))
    qr_s[...] = qr.astype(BF16)
    qd_s[...] = (qr * qdec_ref[...]).astype(BF16)
    _store_chunks_transposed(kr_s, kr)
    _store_chunks_transposed(kd_s, kr * kdec_ref[...])

    gvr = _dot(n, wgvr_ref[...])
    gv_s[...] = gvr[:, :VW].astype(BF16)
    gr_s[...] = gvr[:, VW:]
    rvg = _dot(n, wrvg_ref[...])
    rv_s[...] = rvg[:, :VW].astype(BF16)
    rg_s[...] = rvg[:, VW:]


def _call_b_kernel(h_ref, p_ref, g2_ref, wg_ref, wu_ref, wd_ref, gple_ref, wpg_ref, wpp_ref, gfin_ref,
                   out_ref, *, final):
    h = h_ref[0]
    f = _swiglu(_rms(h, g2_ref[...]).astype(BF16), wg_ref, wu_ref, wd_ref)
    h3 = h + 0.5 * f
    gate = _sigmoid(_dot(_rms(h3, gple_ref[...]).astype(BF16), wpg_ref[...]))
    h4 = h3 + gate * _dot(p_ref[0].astype(BF16), wpp_ref[...])
    out_ref[0] = _rms(h4, gfin_ref[...]) if final else h4


def _const_spec(shape):
    return pl.BlockSpec(shape, lambda *_: (0,) * len(shape), pipeline_mode=pl.Buffered(1))


def _mixer_constants(tile):
    r = np.arange(QK)
    kcol_g = r // DK
    kcol_r = (r % LANES) // (DK // 2)
    row_h = r // CHUNK
    mkg = (row_h[:, None] == kcol_g[None, :])
    mkr = (row_h[:, None] == kcol_r[None, :])
    mv = (row_h[:, None] == (np.arange(VW) // DV)[None, :])
    vrow_h = np.arange(VW) // DV
    mwg = (vrow_h[:, None] == kcol_g[None, :])
    mwr = (vrow_h[:, None] == kcol_r[None, :])
    i = np.arange(CHUNK)
    j = r % CHUNK
    causal = (i[:, None] >= j[None, :]).astype(np.float32)
    t = np.arange(CUM_ROWS)
    ll = np.concatenate([
        (t[:, None] // CHUNK == t[None, :] // CHUNK) & (t[None, :] <= t[:, None]),
        (t[:, None] // CHUNK == t[None, :] // CHUNK)], axis=0)

    log_gamma = np.log(1.0 - 2.0 ** (-5.0 - np.arange(HEADS, dtype=np.float64)))
    pos = np.arange(CHUNK, dtype=np.float64)
    dist = np.abs(pos[:, None] - pos[None, :])
    dmask = np.exp(log_gamma[:, None, None] * dist)
    dmask = dmask.transpose(1, 0, 2).reshape(CHUNK, QK)
    lg_col = log_gamma[kcol_r]
    qdec = np.exp(lg_col[None, :] * (pos + 1.0)[:, None])
    kdec = np.exp(lg_col[None, :] * (CHUNK - 1 - pos)[:, None])
    rdecay = np.tile(np.exp(lg_col * CHUNK)[:, None], (1, DV))
    reps = tile // CHUNK
    as_bf = lambda m: jnp.asarray(m.astype(np.float32), dtype=BF16)
    as_f32 = lambda m: jnp.asarray(m.astype(np.float32))
    return dict(ll=as_bf(ll), mkg=as_bf(mkg.T), mkr=as_bf(mkr.T), mv=as_bf(mv), mwg=as_bf(mwg.T), mwr=as_bf(mwr.T),
                causal=as_f32(causal), dmask=as_f32(dmask), qdec=as_f32(np.tile(qdec, (reps, 1))),
                kdec=as_f32(np.tile(kdec, (reps, 1))), rdecay=as_f32(rdecay))


def _rope_tables(seq):
    half = DK // 2
    inv = ROPE_BASE ** (-np.arange(half, dtype=np.float64) / half)
    ang = np.arange(seq, dtype=np.float64)[:, None] * inv[None, :]
    table = lambda t: jnp.asarray(np.tile(t, (1, HEADS)).astype(np.float32))
    return table(np.cos(ang)), table(np.sin(ang))


def _rope_halves_first(w):
    rows = w.shape[0]
    return w.reshape(rows, HEADS, 2, DK // 2).transpose(0, 2, 1, 3).reshape(rows, QK)


def _call_a(h, cos, sin, consts, g1, wg, wu, wd, gmix, win_parts, wup, bgate, gout, wout):
    b, s, d = h.shape
    tile = TILE_A
    tps = s // tile
    n_tiles = b * tps
    row = lambda a: a.reshape(1, -1)
    args = [h, cos, sin, row(g1), wg, wu, wd, row(gmix), *win_parts, wup, row(bgate), row(gout), wout,
            consts["ll"], consts["mkg"], consts["mkr"], consts["mv"], consts["mwg"], consts["mwr"],
            consts["causal"], consts["dmask"], consts["qdec"], consts["kdec"], consts["rdecay"]]
    proj_tile = lambda st: jnp.minimum(st, n_tiles - 1)
    mix_tile = lambda st: jnp.maximum(st - 1, 0)
    in_specs = [pl.BlockSpec((1, tile, d), lambda st: (proj_tile(st) // tps, proj_tile(st) % tps, 0)),
                pl.BlockSpec((tile, LANES), lambda st: (proj_tile(st) % tps, 0)),
                pl.BlockSpec((tile, LANES), lambda st: (proj_tile(st) % tps, 0))]
    in_specs += [_const_spec(a.shape) for a in args[3:]]
    bf = lambda *shape: pltpu.VMEM(shape, BF16)
    f32 = lambda *shape: pltpu.VMEM(shape, F32)
    kt = bf(tile // CHUNK * QK, 2 * CHUNK)
    scratch = [f32(QK, DV), f32(QK, DV), f32(tile, 2 * VW), f32(tile, d), f32(tile, VW), f32(tile, VW),
               bf(tile, QK), bf(tile, QK), kt, kt, kt, bf(tile, VW),
               f32(tile // CHUNK * QK, 2 * CHUNK),
               bf(tile, QK), bf(tile, QK), kt, kt, bf(tile, VW)]
    return pl.pallas_call(
        functools.partial(_call_a_kernel, tile=tile, tiles_per_seq=tps),
        grid=(n_tiles + 1,),
        in_specs=in_specs,
        out_specs=pl.BlockSpec((1, tile, d), lambda st: (mix_tile(st) // tps, mix_tile(st) % tps, 0)),
        out_shape=jax.ShapeDtypeStruct((b, s, d), F32),
        scratch_shapes=scratch,
        compiler_params=pltpu.CompilerParams(dimension_semantics=("arbitrary",),
                                             vmem_limit_bytes=VMEM_LIMIT),
        name="ffn_a_mixers",
    )(*args)


def _call_b(h, p, g2, wg, wu, wd, gple, wpg, wpp, gfin, final):
    b, s, d = h.shape
    tile = TILE_B
    row = lambda a: a.reshape(1, -1)
    args = [h, p, row(g2), wg, wu, wd, row(gple), wpg, wpp, row(gfin)]
    in_specs = [pl.BlockSpec((1, tile, d), lambda bi, si: (bi, si, 0)),
                pl.BlockSpec((1, tile, p.shape[-1]), lambda bi, si: (bi, si, 0))]
    in_specs += [_const_spec(a.shape) for a in args[2:]]
    return pl.pallas_call(
        functools.partial(_call_b_kernel, final=final),
        grid=(b, s // tile),
        in_specs=in_specs,
        out_specs=pl.BlockSpec((1, tile, d), lambda bi, si: (bi, si, 0)),
        out_shape=jax.ShapeDtypeStruct((b, s, d), F32),
        compiler_params=pltpu.CompilerParams(dimension_semantics=("arbitrary", "arbitrary"),
                                             vmem_limit_bytes=VMEM_LIMIT),
        name="ffn_b_embed",
    )(*args)


def kernel(x, p, g_ffn1, w_ffn1_gate, w_ffn1_up, w_ffn1_down, g_mix, w_in, w_gla_gate_up, b_gla_gate, g_gla_out, g_ret_out, w_out, g_ffn2, w_ffn2_gate, w_ffn2_up, w_ffn2_down, g_ple, w_ple_gate, w_ple_proj, g_final):
    depth = p.shape[0]
    seq = x.shape[1]
    assert seq % TILE_A == 0 and seq % TILE_B == 0 and TILE_A % CHUNK == 0
    assert w_ffn1_gate.shape[2] % MXU_COLS == 0
    consts = _mixer_constants(TILE_A)
    cos, sin = _rope_tables(seq)
    sizes = [QK, QK, VW, VW, GATE_RANK, QK, QK, VW, VW]
    offs = np.concatenate([[0], np.cumsum(sizes)])
    bf = lambda a: a.astype(BF16)
    h = x
    for i in range(depth):
        wi = w_in[i]
        cols = lambda j0, j1: bf(wi[:, offs[j0]:offs[j1]])
        win_parts = [jnp.pad(cols(4, 5), ((0, 0), (0, GATE_RANK_PAD - GATE_RANK))),
                     cols(0, 2),
                     _rope_halves_first(cols(5, 6)), _rope_halves_first(cols(6, 7)),
                     cols(2, 4), cols(7, 9)]
        wup = bf(jnp.pad(w_gla_gate_up[i], ((0, GATE_RANK_PAD - GATE_RANK), (0, 0))))
        gout = jnp.concatenate([g_gla_out[i], g_ret_out[i]])
        h = _call_a(h, cos, sin, consts, g_ffn1[i], bf(w_ffn1_gate[i]), bf(w_ffn1_up[i]), bf(w_ffn1_down[i]),
                    g_mix[i], win_parts, wup, b_gla_gate[i], gout, bf(w_out[i]))
        h = _call_b(h, p[i], g_ffn2[i], bf(w_ffn2_gate[i]), bf(w_ffn2_up[i]), bf(w_ffn2_down[i]),
                    g_ple[i], bf(w_ple_gate[i]), bf(w_ple_proj[i]), g_final, final=(i == depth - 1))
    return h
```

```python
import functools

import jax
import jax.numpy as jnp
import numpy as np
from jax import lax
from jax.experimental import pallas as pl
from jax.experimental.pallas import tpu as pltpu

CHUNK = 64
HEADS = 4
DK = 64
DV = 128
QK = HEADS * DK
VW = HEADS * DV
GATE_RANK = 16
GATE_RANK_PAD = 128
GATE_NORM = 16.0
ROPE_BASE = 10000.0
EPS = 1e-6
LANES = 128
MXU_COLS = 256

TILE_A = 512
CUM_ROWS = 256
TILE_B = 512
STAGE_ROWS = 128
VMEM_LIMIT = 56 * 1024 * 1024

F32 = jnp.float32
BF16 = jnp.bfloat16


def _dot(a, b):
    return jnp.dot(a, b, preferred_element_type=F32)


def _dot_nt(a, b):
    return lax.dot_general(a, b, (((1,), (1,)), ((), ())), preferred_element_type=F32)


def _dot_tn(a, b):
    return lax.dot_general(a, b, (((0,), (0,)), ((), ())), preferred_element_type=F32)


def _rms(x, g):
    return x * lax.rsqrt(jnp.mean(x * x, axis=-1, keepdims=True) + EPS) * g


def _sigmoid(x):
    return 1.0 / (1.0 + jnp.exp(-x))


def _silu(x):
    return x * _sigmoid(x)


def _log_sigmoid(x):
    return jnp.minimum(x, 0.0) - jnp.log1p(jnp.exp(-jnp.abs(x)))


def _swiglu(xn, wg_ref, wu_ref, wd_ref):
    g = _dot(xn, wg_ref[...])
    u = _dot(xn, wu_ref[...])
    return _dot((_silu(g) * u).astype(BF16), wd_ref[...])


def _swiglu_up(xn, wg_ref, wu_ref, j):
    cols = slice(j * MXU_COLS, (j + 1) * MXU_COLS)
    return _dot(xn, wg_ref[:, cols]), _dot(xn, wu_ref[:, cols])


def _swiglu_down(gu, wd_ref, j):
    g, u = gu
    return _dot((_silu(g) * u).astype(BF16), wd_ref[j * MXU_COLS:(j + 1) * MXU_COLS, :])


def _tile4(a):
    return jnp.concatenate([a, a, a, a], axis=0)


def _lanes2(a):
    return jnp.concatenate([a, a], axis=1)


def _lanes4(a):
    return jnp.concatenate([a, a, a, a], axis=1)


def _store_chunks_transposed(dst_ref, k):
    for c in range(k.shape[0] // CHUNK):
        blk = k[c * CHUNK:(c + 1) * CHUNK, :]
        dst_ref[c * QK:(c + 1) * QK, :] = jnp.concatenate([blk, blk], axis=0).T.astype(dst_ref.dtype)


def _call_a_kernel(x_ref, cos_ref, sin_ref, g1_ref, wg_ref, wu_ref, wd_ref, gmix_ref,
                   wlr_ref, wgqk_ref, wrq_ref, wrk_ref, wgvr_ref, wrvg_ref,
                   wup_ref, bgate_ref, gout_ref, wout_ref, ll_ref,
                   mkg_ref, mkr_ref, mv_ref, mwg_ref, mwr_ref, causal_ref, dmask_ref,
                   qdec_ref, kdec_ref, rdecay_ref,
                   out_ref,
                   st_g, st_r, o_s, h1_s, gr_s, rg_s, qf_s, qb_s, kf_s, kb_s, ks_s, gv_s, dec_s,
                   qr_s, qd_s, kr_s, kd_s, rv_s, *, tile, tiles_per_seq):
    step = pl.program_id(0)
    carried = (o_s, h1_s, gr_s, rg_s, qf_s, qb_s, kf_s, kb_s, ks_s, gv_s, dec_s, qr_s, qd_s, kr_s, kd_s, rv_s)

    @pl.when(step == 0)
    def _():
        for ref in (st_g, st_r) + carried:
            ref[...] = jnp.zeros_like(ref)

    keep = lax.rem(jnp.maximum(step - 1, 0), tiles_per_seq) != 0
    st_g[...] = jnp.where(keep, st_g[...], 0.0)
    st_r[...] = jnp.where(keep, st_r[...], 0.0)

    half = DK // 2

    def gla_chunk(c):
        rows = pl.ds(c * CHUNK, CHUNK)
        qf = qf_s[rows, :]
        krows = pl.ds(c * QK, QK)
        s_f = _dot(qf, _lanes2(kf_s[krows, :]) * mkg_ref[...])
        s_b = _dot(qb_s[rows, :], _lanes2(kb_s[krows, :]) * mkg_ref[...])
        p = jnp.where(causal_ref[...] > 0.5, s_f, s_b).astype(BF16)
        v = gv_s[rows, :]
        st = st_g[...]
        w = _lanes4(st.astype(BF16)) * mwg_ref[...]
        o_s[rows, 0:VW] = _dot(p, _tile4(v) * mv_ref[...]) + _dot(qf, w)
        kt = ks_s[krows, :][:, :CHUNK]
        upd = [_dot(kt[h * DK:(h + 1) * DK, :], v[:, h * DV:(h + 1) * DV]) for h in range(HEADS)]
        st_g[...] = st * dec_s[krows, :] + jnp.concatenate(upd, axis=0)

    def ret_chunk(c):
        rows = pl.ds(c * CHUNK, CHUNK)
        s = _dot(qr_s[rows, :], _lanes2(kr_s[pl.ds(c * QK, QK), :]) * mkr_ref[...]) * dmask_ref[...]
        v = rv_s[rows, :]
        st = st_r[...]
        w = _lanes4(st.astype(BF16)) * mwr_ref[...]
        o_s[rows, VW:2 * VW] = _dot(s.astype(BF16), _tile4(v) * mv_ref[...]) + _dot(qd_s[rows, :], w)
        kt = kd_s[pl.ds(c * QK, QK), :][:, :CHUNK]
        upd = []
        for h in range(HEADS):
            k_h = jnp.concatenate([kt[h * half:(h + 1) * half, :],
                                   kt[LANES + h * half:LANES + (h + 1) * half, :]], axis=0)
            upd.append(_dot(k_h, v[:, h * DV:(h + 1) * DV]))
        upd = [u[:half] for u in upd] + [u[half:] for u in upd]
        st_r[...] = st * rdecay_ref[...] + jnp.concatenate(upd, axis=0)

    def mix_operand():
        gout = gout_ref[...]
        gates = (gr_s, rg_s)
        parts = []
        for h in range(2 * HEADS):
            oh = o_s[:, h * DV:(h + 1) * DV]
            gate = gates[h // HEADS][:, (h % HEADS) * DV:(h % HEADS + 1) * DV]
            parts.append((_rms(oh, gout[:, h * DV:(h + 1) * DV]) * _silu(gate)).astype(BF16))
        return jnp.concatenate(parts, axis=1)

    mixer_units = []
    for c in range(tile // CHUNK):
        mixer_units += [functools.partial(gla_chunk, c), functools.partial(ret_chunk, c)]

    mixer_units[0]()
    x = x_ref[0]
    xn = _rms(x, g1_ref[...]).astype(BF16)
    n_pieces = wd_ref.shape[0] // MXU_COLS
    spread = max(n_pieces - 2, 1)
    n_rest = len(mixer_units) - 1
    f = None
    mix = None
    gu = _swiglu_up(xn, wg_ref, wu_ref, 0)
    for j in range(n_pieces):
        gu_next = _swiglu_up(xn, wg_ref, wu_ref, j + 1) if j + 1 < n_pieces else None
        fj = _swiglu_down(gu, wd_ref, j)
        f = fj if f is None else f + fj
        gu = gu_next
        if j < spread:
            for u in range(1 + (j * n_rest) // spread, 1 + ((j + 1) * n_rest) // spread):
                mixer_units[u]()
        if j == spread - 1:
            mix = mix_operand()
    h1 = x + 0.5 * f
    n = _rms(h1, gmix_ref[...]).astype(BF16)

    glr = _dot(n, wlr_ref[...])
    gqk = _dot(n, wgqk_ref[...])
    rq = _dot(n, wrq_ref[...])
    rk = _dot(n, wrk_ref[...])
    out_ref[0] = h1_s[...] + _dot(mix, wout_ref[...])
    gq = gqk[:, :QK]
    gk = gqk[:, QK:]

    logit = _dot(glr.astype(BF16), wup_ref[...]) + bgate_ref[...]
    log_a = _log_sigmoid(logit) * (1.0 / GATE_NORM)
    la_hi = log_a.astype(BF16)
    la_lo = (log_a - la_hi.astype(F32)).astype(BF16)
    ll = ll_ref[...]
    cums, cls = [], []
    for r in range(tile // CUM_ROWS):
        blk = slice(r * CUM_ROWS, (r + 1) * CUM_ROWS)
        cc = _dot(ll, la_hi[blk]) + _dot(ll, la_lo[blk])
        cums.append(cc[:CUM_ROWS])
        cls.append(cc[CUM_ROWS:])
    cum = jnp.concatenate(cums, axis=0)
    cl = jnp.concatenate(cls, axis=0)
    e_pos = jnp.exp(cum)
    e_neg = jnp.exp(-cum)
    qs = gq * (DK ** -0.5)

    cos = cos_ref[...]
    sin = sin_ref[...]

    def rope(t):
        t1 = t[:, :LANES]
        t2 = t[:, LANES:]
        return jnp.concatenate([t1 * cos - t2 * sin, t2 * cos + t1 * sin], axis=1)

    qr = rope(rq) * (DK ** -0.5)
    kr = rope(rk)

    h1_s[...] = h1
    qf_s[...] = (qs * e_pos).astype(BF16)
    qb_s[...] = (qs * e_neg).astype(BF16)
    _store_chunks_transposed(kf_s, gk * e_neg)
    _store_chunks_transposed(kb_s, gk * e_pos)
    _store_chunks_transposed(ks_s, gk * jnp.exp(cl - cum))
    _store_chunks_transposed(dec_s, jnp.exp(cl))
    qr_s[...] = qr.astype(BF16)
    qd_s[...] = (qr * qdec_ref[...]).astype(BF16)
    _store_chunks_transposed(kr_s, kr)
    _store_chunks_transposed(kd_s, kr * kdec_ref[...])

    gvr = _dot(n, wgvr_ref[...])
    gv_s[...] = gvr[:, :VW].astype(BF16)
    gr_s[...] = gvr[:, VW:]
    rvg = _dot(n, wrvg_ref[...])
    rv_s[...] = rvg[:, :VW].astype(BF16)
    rg_s[...] = rvg[:, VW:]


def _stream_weights_bf16(pairs, stage, sem):
    jobs = []
    for src, dst in pairs:
        rows, cols = dst.shape
        for r0 in range(0, rows, STAGE_ROWS):
            jobs.append((src, dst, r0, min(STAGE_ROWS, rows - r0), cols))

    def chunk_copy(j):
        src, _, r0, nr, cols = jobs[j]
        window = stage.at[j % 2, pl.ds(0, nr), pl.ds(0, cols)]
        return pltpu.make_async_copy(src.at[pl.ds(r0, nr), :], window, sem.at[j % 2])

    chunk_copy(0).start()
    for j, (_, dst, r0, nr, cols) in enumerate(jobs):
        if j + 1 < len(jobs):
            chunk_copy(j + 1).start()
        chunk_copy(j).wait()
        dst[pl.ds(r0, nr), :] = stage[j % 2, pl.ds(0, nr), pl.ds(0, cols)].astype(BF16)


def _call_b_kernel(h_ref, p_ref, g2_ref, wg_hbm, wu_hbm, wd_hbm, gple_ref, wpg_hbm, wpp_hbm, gfin_ref,
                   out_ref, wg_ref, wu_ref, wd_ref, wpg_ref, wpp_ref, stage, sem, *, final):
    @pl.when((pl.program_id(0) == 0) & (pl.program_id(1) == 0))
    def _():
        _stream_weights_bf16([(wg_hbm, wg_ref), (wu_hbm, wu_ref), (wd_hbm, wd_ref),
                              (wpg_hbm, wpg_ref), (wpp_hbm, wpp_ref)], stage, sem)

    h = h_ref[0]
    f = _swiglu(_rms(h, g2_ref[...]).astype(BF16), wg_ref, wu_ref, wd_ref)
    h3 = h + 0.5 * f
    gate = _sigmoid(_dot(_rms(h3, gple_ref[...]).astype(BF16), wpg_ref[...]))
    h4 = h3 + gate * _dot(p_ref[0].astype(BF16), wpp_ref[...])
    out_ref[0] = _rms(h4, gfin_ref[...]) if final else h4


def _const_spec(shape):
    return pl.BlockSpec(shape, lambda *_: (0,) * len(shape), pipeline_mode=pl.Buffered(1))


def _mixer_constants(tile):
    r = np.arange(QK)
    kcol_g = r // DK
    kcol_r = (r % LANES) // (DK // 2)
    row_h = r // CHUNK
    mkg = (row_h[:, None] == kcol_g[None, :])
    mkr = (row_h[:, None] == kcol_r[None, :])
    mv = (row_h[:, None] == (np.arange(VW) // DV)[None, :])
    vrow_h = np.arange(VW) // DV
    mwg = (vrow_h[:, None] == kcol_g[None, :])
    mwr = (vrow_h[:, None] == kcol_r[None, :])
    i = np.arange(CHUNK)
    j = r % CHUNK
    causal = (i[:, None] >= j[None, :]).astype(np.float32)
    t = np.arange(CUM_ROWS)
    ll = np.concatenate([
        (t[:, None] // CHUNK == t[None, :] // CHUNK) & (t[None, :] <= t[:, None]),
        (t[:, None] // CHUNK == t[None, :] // CHUNK)], axis=0)

    log_gamma = np.log(1.0 - 2.0 ** (-5.0 - np.arange(HEADS, dtype=np.float64)))
    pos = np.arange(CHUNK, dtype=np.float64)
    dist = np.abs(pos[:, None] - pos[None, :])
    dmask = np.exp(log_gamma[:, None, None] * dist)
    dmask = dmask.transpose(1, 0, 2).reshape(CHUNK, QK)
    lg_col = log_gamma[kcol_r]
    qdec = np.exp(lg_col[None, :] * (pos + 1.0)[:, None])
    kdec = np.exp(lg_col[None, :] * (CHUNK - 1 - pos)[:, None])
    rdecay = np.tile(np.exp(lg_col * CHUNK)[:, None], (1, DV))
    reps = tile // CHUNK
    as_bf = lambda m: jnp.asarray(m.astype(np.float32), dtype=BF16)
    as_f32 = lambda m: jnp.asarray(m.astype(np.float32))
    return dict(ll=as_bf(ll), mkg=as_bf(mkg.T), mkr=as_bf(mkr.T), mv=as_bf(mv), mwg=as_bf(mwg.T), mwr=as_bf(mwr.T),
                causal=as_f32(causal), dmask=as_f32(dmask), qdec=as_f32(np.tile(qdec, (reps, 1))),
                kdec=as_f32(np.tile(kdec, (reps, 1))), rdecay=as_f32(rdecay))


def _rope_tables(seq):
    half = DK // 2
    inv = ROPE_BASE ** (-np.arange(half, dtype=np.float64) / half)
    ang = np.arange(seq, dtype=np.float64)[:, None] * inv[None, :]
    table = lambda t: jnp.asarray(np.tile(t, (1, HEADS)).astype(np.float32))
    return table(np.cos(ang)), table(np.sin(ang))


def _rope_halves_first(w):
    rows = w.shape[0]
    return w.reshape(rows, HEADS, 2, DK // 2).transpose(0, 2, 1, 3).reshape(rows, QK)


def _call_a(h, cos, sin, consts, g1, wg, wu, wd, gmix, win_parts, wup, bgate, gout, wout):
    b, s, d = h.shape
    tile = TILE_A
    tps = s // tile
    n_tiles = b * tps
    row = lambda a: a.reshape(1, -1)
    args = [h, cos, sin, row(g1), wg, wu, wd, row(gmix), *win_parts, wup, row(bgate), row(gout), wout,
            consts["ll"], consts["mkg"], consts["mkr"], consts["mv"], consts["mwg"], consts["mwr"],
            consts["causal"], consts["dmask"], consts["qdec"], consts["kdec"], consts["rdecay"]]
    proj_tile = lambda st: jnp.minimum(st, n_tiles - 1)
    mix_tile = lambda st: jnp.maximum(st - 1, 0)
    in_specs = [pl.BlockSpec((1, tile, d), lambda st: (proj_tile(st) // tps, proj_tile(st) % tps, 0)),
                pl.BlockSpec((tile, LANES), lambda st: (proj_tile(st) % tps, 0)),
                pl.BlockSpec((tile, LANES), lambda st: (proj_tile(st) % tps, 0))]
    in_specs += [_const_spec(a.shape) for a in args[3:]]
    bf = lambda *shape: pltpu.VMEM(shape, BF16)
    f32 = lambda *shape: pltpu.VMEM(shape, F32)
    kt = bf(tile // CHUNK * QK, 2 * CHUNK)
    scratch = [f32(QK, DV), f32(QK, DV), f32(tile, 2 * VW), f32(tile, d), f32(tile, VW), f32(tile, VW),
               bf(tile, QK), bf(tile, QK), kt, kt, kt, bf(tile, VW),
               f32(tile // CHUNK * QK, 2 * CHUNK),
               bf(tile, QK), bf(tile, QK), kt, kt, bf(tile, VW)]
    return pl.pallas_call(
        functools.partial(_call_a_kernel, tile=tile, tiles_per_seq=tps),
        grid=(n_tiles + 1,),
        in_specs=in_specs,
        out_specs=pl.BlockSpec((1, tile, d), lambda st: (mix_tile(st) // tps, mix_tile(st) % tps, 0)),
        out_shape=jax.ShapeDtypeStruct((b, s, d), F32),
        scratch_shapes=scratch,
        compiler_params=pltpu.CompilerParams(dimension_semantics=("arbitrary",),
                                             vmem_limit_bytes=VMEM_LIMIT),
        name="ffn_a_mixers",
    )(*args)


def _call_b(h, p, g2, wg, wu, wd, gple, wpg, wpp, gfin, final):
    b, s, d = h.shape
    tile = TILE_B
    row = lambda a: a.reshape(1, -1)
    weights = (wg, wu, wd, wpg, wpp)
    args = [h, p, row(g2), wg, wu, wd, row(gple), wpg, wpp, row(gfin)]
    in_specs = [pl.BlockSpec((1, tile, d), lambda bi, si: (bi, si, 0)),
                pl.BlockSpec((1, tile, p.shape[-1]), lambda bi, si: (bi, si, 0))]
    in_specs += [pl.BlockSpec(memory_space=pl.ANY) if any(a is w for w in weights) else _const_spec(a.shape)
                 for a in args[2:]]
    scratch = [pltpu.VMEM(w.shape, BF16) for w in weights]
    scratch += [pltpu.VMEM((2, STAGE_ROWS, max(w.shape[1] for w in weights)), F32), pltpu.SemaphoreType.DMA((2,))]
    return pl.pallas_call(
        functools.partial(_call_b_kernel, final=final),
        grid=(b, s // tile),
        in_specs=in_specs,
        out_specs=pl.BlockSpec((1, tile, d), lambda bi, si: (bi, si, 0)),
        out_shape=jax.ShapeDtypeStruct((b, s, d), F32),
        scratch_shapes=scratch,
        compiler_params=pltpu.CompilerParams(dimension_semantics=("arbitrary", "arbitrary"),
                                             vmem_limit_bytes=VMEM_LIMIT),
        name="ffn_b_embed",
    )(*args)


def kernel(x, p, g_ffn1, w_ffn1_gate, w_ffn1_up, w_ffn1_down, g_mix, w_in, w_gla_gate_up, b_gla_gate, g_gla_out, g_ret_out, w_out, g_ffn2, w_ffn2_gate, w_ffn2_up, w_ffn2_down, g_ple, w_ple_gate, w_ple_proj, g_final):
    depth = p.shape[0]
    seq = x.shape[1]
    assert seq % TILE_A == 0 and seq % TILE_B == 0 and TILE_A % CHUNK == 0
    assert w_ffn1_gate.shape[2] % MXU_COLS == 0
    consts = _mixer_constants(TILE_A)
    cos, sin = _rope_tables(seq)
    sizes = [QK, QK, VW, VW, GATE_RANK, QK, QK, VW, VW]
    offs = np.concatenate([[0], np.cumsum(sizes)])
    bf = lambda a: a.astype(BF16)
    h = x
    for i in range(depth):
        wi = w_in[i]
        cols = lambda j0, j1: bf(wi[:, offs[j0]:offs[j1]])
        win_parts = [jnp.pad(cols(4, 5), ((0, 0), (0, GATE_RANK_PAD - GATE_RANK))),
                     cols(0, 2),
                     _rope_halves_first(cols(5, 6)), _rope_halves_first(cols(6, 7)),
                     cols(2, 4), cols(7, 9)]
        wup = bf(jnp.pad(w_gla_gate_up[i], ((0, GATE_RANK_PAD - GATE_RANK), (0, 0))))
        gout = jnp.concatenate([g_gla_out[i], g_ret_out[i]])
        h = _call_a(h, cos, sin, consts, g_ffn1[i], bf(w_ffn1_gate[i]), bf(w_ffn1_up[i]), bf(w_ffn1_down[i]),
                    g_mix[i], win_parts, wup, b_gla_gate[i], gout, bf(w_out[i]))
        h = _call_b(h, p[i], g_ffn2[i], w_ffn2_gate[i], w_ffn2_up[i], w_ffn2_down[i],
                    g_ple[i], w_ple_gate[i], w_ple_proj[i], g_final, final=(i == depth - 1))
    return h
```

```python
import functools

import jax
import jax.numpy as jnp
import numpy as np
from jax import lax
from jax.experimental import pallas as pl
from jax.experimental.pallas import tpu as pltpu

CHUNK = 64
HEADS = 4
DK = 64
DV = 128
QK = HEADS * DK
VW = HEADS * DV
GATE_RANK = 16
GATE_RANK_PAD = 128
GATE_NORM = 16.0
ROPE_BASE = 10000.0
EPS = 1e-6
LANES = 128
BF16_SUBLANES = 16
MXU_COLS = 256

TILE_A = 512
CUM_ROWS = 256
TILE_B = 512
VMEM_LIMIT = 60 * 1024 * 1024

F32 = jnp.float32
BF16 = jnp.bfloat16


def _dot(a, b):
    return jnp.dot(a, b, preferred_element_type=F32)


def _dot_nt(a, b):
    return lax.dot_general(a, b, (((1,), (1,)), ((), ())), preferred_element_type=F32)


def _dot_tn(a, b):
    return lax.dot_general(a, b, (((0,), (0,)), ((), ())), preferred_element_type=F32)


def _rms(x, g):
    return x * lax.rsqrt(jnp.mean(x * x, axis=-1, keepdims=True) + EPS) * g


def _sigmoid(x):
    return 1.0 / (1.0 + jnp.exp(-x))


def _silu(x):
    return x * _sigmoid(x)


def _log_sigmoid(x):
    return jnp.minimum(x, 0.0) - jnp.log1p(jnp.exp(-jnp.abs(x)))


def _swiglu(xn, wg_ref, wu_ref, wd_ref):
    g = _dot(xn, wg_ref[...])
    u = _dot(xn, wu_ref[...])
    return _dot((_silu(g) * u).astype(BF16), wd_ref[...])


def _swiglu_up(xn, wg_ref, wu_ref, j):
    cols = slice(j * MXU_COLS, (j + 1) * MXU_COLS)
    return _dot(xn, wg_ref[:, cols]), _dot(xn, wu_ref[:, cols])


def _swiglu_down(gu, wd_ref, j):
    g, u = gu
    return _dot((_silu(g) * u).astype(BF16), wd_ref[j * MXU_COLS:(j + 1) * MXU_COLS, :])


def _tile4(a):
    return jnp.concatenate([a, a, a, a], axis=0)


def _lanes2(a):
    return jnp.concatenate([a, a], axis=1)


def _lanes4(a):
    return jnp.concatenate([a, a, a, a], axis=1)


def _store_chunks_transposed(dst_ref, k):
    for c in range(k.shape[0] // CHUNK):
        blk = k[c * CHUNK:(c + 1) * CHUNK, :]
        dst_ref[c * QK:(c + 1) * QK, :] = jnp.concatenate([blk, blk], axis=0).T.astype(dst_ref.dtype)


def _call_a_kernel(x_ref, cos_ref, sin_ref, g1_ref, wg_ref, wu_ref, wd_ref, gmix_ref,
                   wlr_ref, wgqk_ref, wrq_ref, wrk_ref, wgvr_ref, wrvg_ref,
                   wup_ref, bgate_ref, gout_ref, wout_ref, ll_ref,
                   mkg_ref, mkr_ref, mv_ref, mwg_ref, mwr_ref, causal_ref, dmask_ref,
                   qdec_ref, kdec_ref, rdecay_ref,
                   out_ref,
                   st_g, st_r, o_s, h1_s, gr_s, rg_s, qf_s, qb_s, kf_s, kb_s, ks_s, gv_s, dec_s,
                   qr_s, qd_s, kr_s, kd_s, rv_s, *, tile, tiles_per_seq):
    step = pl.program_id(0)
    carried = (o_s, h1_s, gr_s, rg_s, qf_s, qb_s, kf_s, kb_s, ks_s, gv_s, dec_s, qr_s, qd_s, kr_s, kd_s, rv_s)

    @pl.when(step == 0)
    def _():
        for ref in (st_g, st_r) + carried:
            ref[...] = jnp.zeros_like(ref)

    keep = lax.rem(jnp.maximum(step - 1, 0), tiles_per_seq) != 0
    st_g[...] = jnp.where(keep, st_g[...], 0.0)
    st_r[...] = jnp.where(keep, st_r[...], 0.0)

    half = DK // 2

    def gla_chunk(c):
        rows = pl.ds(c * CHUNK, CHUNK)
        qf = qf_s[rows, :]
        krows = pl.ds(c * QK, QK)
        s_f = _dot(qf, _lanes2(kf_s[krows, :]) * mkg_ref[...])
        s_b = _dot(qb_s[rows, :], _lanes2(kb_s[krows, :]) * mkg_ref[...])
        p = jnp.where(causal_ref[...] > 0.5, s_f, s_b).astype(BF16)
        v = gv_s[rows, :]
        st = st_g[...]
        w = _lanes4(st.astype(BF16)) * mwg_ref[...]
        o_s[rows, 0:VW] = _dot(p, _tile4(v) * mv_ref[...]) + _dot(qf, w)
        kt = ks_s[krows, :][:, :CHUNK]
        upd = [_dot(kt[h * DK:(h + 1) * DK, :], v[:, h * DV:(h + 1) * DV]) for h in range(HEADS)]
        st_g[...] = st * dec_s[krows, :] + jnp.concatenate(upd, axis=0)

    def ret_chunk(c):
        rows = pl.ds(c * CHUNK, CHUNK)
        s = _dot(qr_s[rows, :], _lanes2(kr_s[pl.ds(c * QK, QK), :]) * mkr_ref[...]) * dmask_ref[...]
        v = rv_s[rows, :]
        st = st_r[...]
        w = _lanes4(st.astype(BF16)) * mwr_ref[...]
        o_s[rows, VW:2 * VW] = _dot(s.astype(BF16), _tile4(v) * mv_ref[...]) + _dot(qd_s[rows, :], w)
        kt = kd_s[pl.ds(c * QK, QK), :][:, :CHUNK]
        upd = []
        for h in range(HEADS):
            k_h = jnp.concatenate([kt[h * half:(h + 1) * half, :],
                                   kt[LANES + h * half:LANES + (h + 1) * half, :]], axis=0)
            upd.append(_dot(k_h, v[:, h * DV:(h + 1) * DV]))
        upd = [u[:half] for u in upd] + [u[half:] for u in upd]
        st_r[...] = st * rdecay_ref[...] + jnp.concatenate(upd, axis=0)

    def mix_operand():
        gout = gout_ref[...]
        gates = (gr_s, rg_s)
        parts = []
        for h in range(2 * HEADS):
            oh = o_s[:, h * DV:(h + 1) * DV]
            gate = gates[h // HEADS][:, (h % HEADS) * DV:(h % HEADS + 1) * DV]
            parts.append((_rms(oh, gout[:, h * DV:(h + 1) * DV]) * _silu(gate)).astype(BF16))
        return jnp.concatenate(parts, axis=1)

    mixer_units = []
    for c in range(tile // CHUNK):
        mixer_units += [functools.partial(gla_chunk, c), functools.partial(ret_chunk, c)]

    mixer_units[0]()
    x = x_ref[0]
    xn = _rms(x, g1_ref[...]).astype(BF16)
    n_pieces = wd_ref.shape[0] // MXU_COLS
    spread = max(n_pieces - 2, 1)
    n_rest = len(mixer_units) - 1
    f = None
    mix = None
    gu = _swiglu_up(xn, wg_ref, wu_ref, 0)
    for j in range(n_pieces):
        gu_next = _swiglu_up(xn, wg_ref, wu_ref, j + 1) if j + 1 < n_pieces else None
        fj = _swiglu_down(gu, wd_ref, j)
        f = fj if f is None else f + fj
        gu = gu_next
        if j < spread:
            for u in range(1 + (j * n_rest) // spread, 1 + ((j + 1) * n_rest) // spread):
                mixer_units[u]()
        if j == spread - 1:
            mix = mix_operand()
    h1 = x + 0.5 * f
    n = _rms(h1, gmix_ref[...]).astype(BF16)

    glr = _dot(n, wlr_ref[...])
    gqk = _dot(n, wgqk_ref[...])
    rq = _dot(n, wrq_ref[...])
    rk = _dot(n, wrk_ref[...])
    out_ref[0] = h1_s[...] + _dot(mix, wout_ref[...])
    gq = gqk[:, :QK]
    gk = gqk[:, QK:]

    logit = _dot(glr.astype(BF16), wup_ref[...]) + bgate_ref[...]
    log_a = _log_sigmoid(logit) * (1.0 / GATE_NORM)
    la_hi = log_a.astype(BF16)
    la_lo = (log_a - la_hi.astype(F32)).astype(BF16)
    ll = ll_ref[...]
    cums, cls = [], []
    for r in range(tile // CUM_ROWS):
        blk = slice(r * CUM_ROWS, (r + 1) * CUM_ROWS)
        cc = _dot(ll, la_hi[blk]) + _dot(ll, la_lo[blk])
        cums.append(cc[:CUM_ROWS])
        cls.append(cc[CUM_ROWS:])
    cum = jnp.concatenate(cums, axis=0)
    cl = jnp.concatenate(cls, axis=0)
    e_pos = jnp.exp(cum)
    e_neg = jnp.exp(-cum)
    qs = gq * (DK ** -0.5)

    cos = cos_ref[...]
    sin = sin_ref[...]

    def rope(t):
        t1 = t[:, :LANES]
        t2 = t[:, LANES:]
        return jnp.concatenate([t1 * cos - t2 * sin, t2 * cos + t1 * sin], axis=1)

    qr = rope(rq) * (DK ** -0.5)
    kr = rope(rk)

    h1_s[...] = h1
    qf_s[...] = (qs * e_pos).astype(BF16)
    qb_s[...] = (qs * e_neg).astype(BF16)
    _store_chunks_transposed(kf_s, gk * e_neg)
    _store_chunks_transposed(kb_s, gk * e_pos)
    _store_chunks_transposed(ks_s, gk * jnp.exp(cl - cum))
    _store_chunks_transposed(dec_s, jnp.exp(cl))
    qr_s[...] = qr.astype(BF16)
    qd_s[...] = (qr * qdec_ref[...]).astype(BF16)
    _store_chunks_transposed(kr_s, kr)
    _store_chunks_transposed(kd_s, kr * kdec_ref[...])

    gvr = _dot(n, wgvr_ref[...])
    gv_s[...] = gvr[:, :VW].astype(BF16)
    gr_s[...] = gvr[:, VW:]
    rvg = _dot(n, wrvg_ref[...])
    rv_s[...] = rvg[:, :VW].astype(BF16)
    rg_s[...] = rvg[:, VW:]


def _call_b_kernel(h_ref, p_ref, g2_ref, wg_ref, wu_ref, wd_ref, gple_ref, wpg_ref, wpp_ref, gfin_ref,
                   out_ref, *, final):
    h = h_ref[0]
    f = _swiglu(_rms(h, g2_ref[...]).astype(BF16), wg_ref, wu_ref, wd_ref)
    h3 = h + 0.5 * f
    gate = _sigmoid(_dot(_rms(h3, gple_ref[...]).astype(BF16), wpg_ref[...]))
    h4 = h3 + gate * _dot(p_ref[0].astype(BF16), wpp_ref[...])
    out_ref[0] = _rms(h4, gfin_ref[...]) if final else h4


def _const_spec(shape):
    return pl.BlockSpec(shape, lambda *_: (0,) * len(shape), pipeline_mode=pl.Buffered(1))


def _mixer_constants(tile):
    r = np.arange(QK)
    kcol_g = r // DK
    kcol_r = (r % LANES) // (DK // 2)
    row_h = r // CHUNK
    mkg = (row_h[:, None] == kcol_g[None, :])
    mkr = (row_h[:, None] == kcol_r[None, :])
    mv = (row_h[:, None] == (np.arange(VW) // DV)[None, :])
    vrow_h = np.arange(VW) // DV
    mwg = (vrow_h[:, None] == kcol_g[None, :])
    mwr = (vrow_h[:, None] == kcol_r[None, :])
    i = np.arange(CHUNK)
    j = r % CHUNK
    causal = (i[:, None] >= j[None, :]).astype(np.float32)
    t = np.arange(CUM_ROWS)
    ll = np.concatenate([
        (t[:, None] // CHUNK == t[None, :] // CHUNK) & (t[None, :] <= t[:, None]),
        (t[:, None] // CHUNK == t[None, :] // CHUNK)], axis=0)

    log_gamma = np.log(1.0 - 2.0 ** (-5.0 - np.arange(HEADS, dtype=np.float64)))
    pos = np.arange(CHUNK, dtype=np.float64)
    dist = np.abs(pos[:, None] - pos[None, :])
    dmask = np.exp(log_gamma[:, None, None] * dist)
    dmask = dmask.transpose(1, 0, 2).reshape(CHUNK, QK)
    lg_col = log_gamma[kcol_r]
    qdec = np.exp(lg_col[None, :] * (pos + 1.0)[:, None])
    kdec = np.exp(lg_col[None, :] * (CHUNK - 1 - pos)[:, None])
    rdecay = np.tile(np.exp(lg_col * CHUNK)[:, None], (1, DV))
    reps = tile // CHUNK
    as_bf = lambda m: jnp.asarray(m.astype(np.float32), dtype=BF16)
    as_f32 = lambda m: jnp.asarray(m.astype(np.float32))
    return dict(ll=as_bf(ll), mkg=as_bf(mkg.T), mkr=as_bf(mkr.T), mv=as_bf(mv), mwg=as_bf(mwg.T), mwr=as_bf(mwr.T),
                causal=as_f32(causal), dmask=as_f32(dmask), qdec=as_f32(np.tile(qdec, (reps, 1))),
                kdec=as_f32(np.tile(kdec, (reps, 1))), rdecay=as_f32(rdecay))


def _rope_tables(seq):
    half = DK // 2
    inv = ROPE_BASE ** (-np.arange(half, dtype=np.float64) / half)
    ang = np.arange(seq, dtype=np.float64)[:, None] * inv[None, :]
    table = lambda t: jnp.asarray(np.tile(t, (1, HEADS)).astype(np.float32))
    return table(np.cos(ang)), table(np.sin(ang))


def _rope_halves_first(w):
    rows = w.shape[0]
    return w.reshape(rows, HEADS, 2, DK // 2).transpose(0, 2, 1, 3).reshape(rows, QK)


def _call_a_with_casts_kernel(*refs, n_in, n_cast, **kw):
    ins, cast_in = refs[:n_in], refs[n_in:n_in + n_cast]
    out_ref, cast_out = refs[n_in + n_cast], refs[n_in + n_cast + 1:n_in + 2 * n_cast + 1]
    scratch = refs[n_in + 2 * n_cast + 1:]
    for src, dst in zip(cast_in, cast_out):
        dst[...] = src[...].astype(BF16)
    _call_a_kernel(*ins, out_ref, *scratch, **kw)


def _cast_block_rows(rows, max_steps):
    for block in range(BF16_SUBLANES, rows + 1, BF16_SUBLANES):
        if rows % block == 0 and rows // block <= max_steps:
            return block
    raise ValueError(f"no row block for {rows} rows in {max_steps} steps")


def _call_a(h, cos, sin, consts, g1, wg, wu, wd, gmix, win_parts, wup, bgate, gout, wout, cast_weights):
    b, s, d = h.shape
    tile = TILE_A
    tps = s // tile
    n_tiles = b * tps
    row = lambda a: a.reshape(1, -1)
    args = [h, cos, sin, row(g1), wg, wu, wd, row(gmix), *win_parts, wup, row(bgate), row(gout), wout,
            consts["ll"], consts["mkg"], consts["mkr"], consts["mv"], consts["mwg"], consts["mwr"],
            consts["causal"], consts["dmask"], consts["qdec"], consts["kdec"], consts["rdecay"]]
    proj_tile = lambda st: jnp.minimum(st, n_tiles - 1)
    mix_tile = lambda st: jnp.maximum(st - 1, 0)
    in_specs = [pl.BlockSpec((1, tile, d), lambda st: (proj_tile(st) // tps, proj_tile(st) % tps, 0)),
                pl.BlockSpec((tile, LANES), lambda st: (proj_tile(st) % tps, 0)),
                pl.BlockSpec((tile, LANES), lambda st: (proj_tile(st) % tps, 0))]
    in_specs += [_const_spec(a.shape) for a in args[3:]]
    bf = lambda *shape: pltpu.VMEM(shape, BF16)
    f32 = lambda *shape: pltpu.VMEM(shape, F32)
    kt = bf(tile // CHUNK * QK, 2 * CHUNK)
    scratch = [f32(QK, DV), f32(QK, DV), f32(tile, 2 * VW), f32(tile, d), f32(tile, VW), f32(tile, VW),
               bf(tile, QK), bf(tile, QK), kt, kt, kt, bf(tile, VW),
               f32(tile // CHUNK * QK, 2 * CHUNK),
               bf(tile, QK), bf(tile, QK), kt, kt, bf(tile, VW)]
    out_specs = [pl.BlockSpec((1, tile, d), lambda st: (mix_tile(st) // tps, mix_tile(st) % tps, 0))]
    out_shape = [jax.ShapeDtypeStruct((b, s, d), F32)]
    n_in = len(args)
    for w in cast_weights:
        rows, cols = w.shape
        block = _cast_block_rows(rows, n_tiles)
        spec = pl.BlockSpec((block, cols), functools.partial(lambda st, last: (jnp.minimum(st, last), 0),
                                                             last=rows // block - 1))
        in_specs.append(spec)
        out_specs.append(spec)
        out_shape.append(jax.ShapeDtypeStruct(w.shape, BF16))
    outs = pl.pallas_call(
        functools.partial(_call_a_with_casts_kernel, n_in=n_in, n_cast=len(cast_weights),
                          tile=tile, tiles_per_seq=tps),
        grid=(n_tiles + 1,),
        in_specs=in_specs,
        out_specs=out_specs,
        out_shape=out_shape,
        scratch_shapes=scratch,
        compiler_params=pltpu.CompilerParams(dimension_semantics=("arbitrary",),
                                             vmem_limit_bytes=VMEM_LIMIT),
        name="ffn_a_mixers",
    )(*args, *cast_weights)
    return outs[0], outs[1:]


def _call_b(h, p, g2, wg, wu, wd, gple, wpg, wpp, gfin, final):
    b, s, d = h.shape
    tile = TILE_B
    row = lambda a: a.reshape(1, -1)
    args = [h, p, row(g2), wg, wu, wd, row(gple), wpg, wpp, row(gfin)]
    in_specs = [pl.BlockSpec((1, tile, d), lambda bi, si: (bi, si, 0)),
                pl.BlockSpec((1, tile, p.shape[-1]), lambda bi, si: (bi, si, 0))]
    in_specs += [_const_spec(a.shape) for a in args[2:]]
    return pl.pallas_call(
        functools.partial(_call_b_kernel, final=final),
        grid=(b, s // tile),
        in_specs=in_specs,
        out_specs=pl.BlockSpec((1, tile, d), lambda bi, si: (bi, si, 0)),
        out_shape=jax.ShapeDtypeStruct((b, s, d), F32),
        compiler_params=pltpu.CompilerParams(dimension_semantics=("arbitrary", "arbitrary"),
                                             vmem_limit_bytes=VMEM_LIMIT),
        name="ffn_b_embed",
    )(*args)


def kernel(x, p, g_ffn1, w_ffn1_gate, w_ffn1_up, w_ffn1_down, g_mix, w_in, w_gla_gate_up, b_gla_gate, g_gla_out, g_ret_out, w_out, g_ffn2, w_ffn2_gate, w_ffn2_up, w_ffn2_down, g_ple, w_ple_gate, w_ple_proj, g_final):
    depth = p.shape[0]
    seq = x.shape[1]
    assert seq % TILE_A == 0 and seq % TILE_B == 0 and TILE_A % CHUNK == 0
    assert w_ffn1_gate.shape[2] % MXU_COLS == 0
    consts = _mixer_constants(TILE_A)
    cos, sin = _rope_tables(seq)
    sizes = [QK, QK, VW, VW, GATE_RANK, QK, QK, VW, VW]
    offs = np.concatenate([[0], np.cumsum(sizes)])
    bf = lambda a: a.astype(BF16)
    h = x
    for i in range(depth):
        wi = w_in[i]
        cols = lambda j0, j1: bf(wi[:, offs[j0]:offs[j1]])
        win_parts = [jnp.pad(cols(4, 5), ((0, 0), (0, GATE_RANK_PAD - GATE_RANK))),
                     cols(0, 2),
                     _rope_halves_first(cols(5, 6)), _rope_halves_first(cols(6, 7)),
                     cols(2, 4), cols(7, 9)]
        wup = bf(jnp.pad(w_gla_gate_up[i], ((0, GATE_RANK_PAD - GATE_RANK), (0, 0))))
        gout = jnp.concatenate([g_gla_out[i], g_ret_out[i]])
        next_weights = [w_ffn2_gate[i], w_ffn2_up[i], w_ffn2_down[i], w_ple_gate[i], w_ple_proj[i]]
        h, (wg2, wu2, wd2, wpg, wpp) = _call_a(
            h, cos, sin, consts, g_ffn1[i], bf(w_ffn1_gate[i]), bf(w_ffn1_up[i]), bf(w_ffn1_down[i]),
            g_mix[i], win_parts, wup, b_gla_gate[i], gout, bf(w_out[i]), next_weights)
        h = _call_b(h, p[i], g_ffn2[i], wg2, wu2, wd2, g_ple[i], wpg, wpp, g_final, final=(i == depth - 1))
    return h
```

```python
import functools

import jax
import jax.numpy as jnp
import numpy as np
from jax import lax
from jax.experimental import pallas as pl
from jax.experimental.pallas import tpu as pltpu

CHUNK = 64
HEADS = 4
DK = 64
DV = 128
QK = HEADS * DK
VW = HEADS * DV
GATE_RANK = 16
GATE_RANK_PAD = 128
GATE_NORM = 16.0
ROPE_BASE = 10000.0
EPS = 1e-6
LANES = 128
BF16_SUBLANES = 16
MXU_COLS = 256

TILE_A = 512
CUM_ROWS = 256
TILE_B = 512
VMEM_LIMIT = 60 * 1024 * 1024

F32 = jnp.float32
BF16 = jnp.bfloat16


def _dot(a, b):
    return jnp.dot(a, b, preferred_element_type=F32)


def _dot_nt(a, b):
    return lax.dot_general(a, b, (((1,), (1,)), ((), ())), preferred_element_type=F32)


def _dot_tn(a, b):
    return lax.dot_general(a, b, (((0,), (0,)), ((), ())), preferred_element_type=F32)


def _rms(x, g):
    return x * lax.rsqrt(jnp.mean(x * x, axis=-1, keepdims=True) + EPS) * g


def _sigmoid(x):
    return 1.0 / (1.0 + jnp.exp(-x))


def _silu(x):
    return x * _sigmoid(x)


def _log_sigmoid(x):
    return jnp.minimum(x, 0.0) - jnp.log1p(jnp.exp(-jnp.abs(x)))


def _swiglu(xn, wg_ref, wu_ref, wd_ref):
    g = _dot(xn, wg_ref[...])
    u = _dot(xn, wu_ref[...])
    return _dot((_silu(g) * u).astype(BF16), wd_ref[...])


def _swiglu_up(xn, wg_ref, wu_ref, j):
    cols = slice(j * MXU_COLS, (j + 1) * MXU_COLS)
    return _dot(xn, wg_ref[:, cols]), _dot(xn, wu_ref[:, cols])


def _swiglu_down(gu, wd_ref, j):
    g, u = gu
    return _dot((_silu(g) * u).astype(BF16), wd_ref[j * MXU_COLS:(j + 1) * MXU_COLS, :])


def _tile4(a):
    return jnp.concatenate([a, a, a, a], axis=0)


def _lanes2(a):
    return jnp.concatenate([a, a], axis=1)


def _lanes4(a):
    return jnp.concatenate([a, a, a, a], axis=1)


def _store_chunks_transposed(dst_ref, k):
    for c in range(k.shape[0] // CHUNK):
        blk = k[c * CHUNK:(c + 1) * CHUNK, :]
        dst_ref[c * QK:(c + 1) * QK, :] = jnp.concatenate([blk, blk], axis=0).T.astype(dst_ref.dtype)


def _call_a_kernel(x_ref, cos_ref, sin_ref, g1_ref, wg_ref, wu_ref, wd_ref, gmix_ref,
                   wlr_ref, wgqk_ref, wrq_ref, wrk_ref, wgvr_ref, wrvg_ref,
                   wup_ref, bgate_ref, gout_ref, wout_ref, ll_ref,
                   mkg_ref, mkr_ref, mv_ref, mwg_ref, mwr_ref, causal_ref, dmask_ref,
                   qdec_ref, kdec_ref, rdecay_ref,
                   out_ref,
                   st_g, st_r, o_s, h1_s, gr_s, rg_s, qf_s, qb_s, kf_s, kb_s, ks_s, gv_s, dec_s,
                   qr_s, qd_s, kr_s, kd_s, rv_s, *, tile, tiles_per_seq, flush):
    step = pl.program_id(0)
    carried = (o_s, h1_s, gr_s, rg_s, qf_s, qb_s, kf_s, kb_s, ks_s, gv_s, dec_s, qr_s, qd_s, kr_s, kd_s, rv_s)

    @pl.when(step == 0)
    def _():
        for ref in (st_g, st_r) + carried:
            ref[...] = jnp.zeros_like(ref)

    keep = lax.rem(jnp.maximum(step - 1, 0), tiles_per_seq) != 0
    st_g[...] = jnp.where(keep, st_g[...], 0.0)
    st_r[...] = jnp.where(keep, st_r[...], 0.0)

    half = DK // 2

    def gla_chunk(c):
        rows = pl.ds(c * CHUNK, CHUNK)
        qf = qf_s[rows, :]
        krows = pl.ds(c * QK, QK)
        s_f = _dot(qf, _lanes2(kf_s[krows, :]) * mkg_ref[...])
        s_b = _dot(qb_s[rows, :], _lanes2(kb_s[krows, :]) * mkg_ref[...])
        p = jnp.where(causal_ref[...] > 0.5, s_f, s_b).astype(BF16)
        v = gv_s[rows, :]
        st = st_g[...]
        w = _lanes4(st.astype(BF16)) * mwg_ref[...]
        o_s[rows, 0:VW] = _dot(p, _tile4(v) * mv_ref[...]) + _dot(qf, w)
        kt = ks_s[krows, :][:, :CHUNK]
        upd = [_dot(kt[h * DK:(h + 1) * DK, :], v[:, h * DV:(h + 1) * DV]) for h in range(HEADS)]
        st_g[...] = st * dec_s[krows, :] + jnp.concatenate(upd, axis=0)

    def ret_chunk(c):
        rows = pl.ds(c * CHUNK, CHUNK)
        s = _dot(qr_s[rows, :], _lanes2(kr_s[pl.ds(c * QK, QK), :]) * mkr_ref[...]) * dmask_ref[...]
        v = rv_s[rows, :]
        st = st_r[...]
        w = _lanes4(st.astype(BF16)) * mwr_ref[...]
        o_s[rows, VW:2 * VW] = _dot(s.astype(BF16), _tile4(v) * mv_ref[...]) + _dot(qd_s[rows, :], w)
        kt = kd_s[pl.ds(c * QK, QK), :][:, :CHUNK]
        upd = []
        for h in range(HEADS):
            k_h = jnp.concatenate([kt[h * half:(h + 1) * half, :],
                                   kt[LANES + h * half:LANES + (h + 1) * half, :]], axis=0)
            upd.append(_dot(k_h, v[:, h * DV:(h + 1) * DV]))
        upd = [u[:half] for u in upd] + [u[half:] for u in upd]
        st_r[...] = st * rdecay_ref[...] + jnp.concatenate(upd, axis=0)

    def mix_operand():
        gout = gout_ref[...]
        gates = (gr_s, rg_s)
        parts = []
        for h in range(2 * HEADS):
            oh = o_s[:, h * DV:(h + 1) * DV]
            gate = gates[h // HEADS][:, (h % HEADS) * DV:(h % HEADS + 1) * DV]
            parts.append((_rms(oh, gout[:, h * DV:(h + 1) * DV]) * _silu(gate)).astype(BF16))
        return jnp.concatenate(parts, axis=1)

    mixer_units = []
    for c in range(tile // CHUNK):
        mixer_units += [functools.partial(gla_chunk, c), functools.partial(ret_chunk, c)]

    if flush:
        for unit in mixer_units:
            unit()
        out_ref[0] = h1_s[...] + _dot(mix_operand(), wout_ref[...])
        return

    mixer_units[0]()
    x = x_ref[0]
    xn = _rms(x, g1_ref[...]).astype(BF16)
    n_pieces = wd_ref.shape[0] // MXU_COLS
    spread = max(n_pieces - 2, 1)
    n_rest = len(mixer_units) - 1
    f = None
    mix = None
    gu = _swiglu_up(xn, wg_ref, wu_ref, 0)
    for j in range(n_pieces):
        gu_next = _swiglu_up(xn, wg_ref, wu_ref, j + 1) if j + 1 < n_pieces else None
        fj = _swiglu_down(gu, wd_ref, j)
        f = fj if f is None else f + fj
        gu = gu_next
        if j < spread:
            for u in range(1 + (j * n_rest) // spread, 1 + ((j + 1) * n_rest) // spread):
                mixer_units[u]()
        if j == spread - 1:
            mix = mix_operand()
    h1 = x + 0.5 * f
    n = _rms(h1, gmix_ref[...]).astype(BF16)

    glr = _dot(n, wlr_ref[...])
    gqk = _dot(n, wgqk_ref[...])
    rq = _dot(n, wrq_ref[...])
    rk = _dot(n, wrk_ref[...])
    out_ref[0] = h1_s[...] + _dot(mix, wout_ref[...])
    gq = gqk[:, :QK]
    gk = gqk[:, QK:]

    logit = _dot(glr.astype(BF16), wup_ref[...]) + bgate_ref[...]
    log_a = _log_sigmoid(logit) * (1.0 / GATE_NORM)
    la_hi = log_a.astype(BF16)
    la_lo = (log_a - la_hi.astype(F32)).astype(BF16)
    ll = ll_ref[...]
    cums, cls = [], []
    for r in range(tile // CUM_ROWS):
        blk = slice(r * CUM_ROWS, (r + 1) * CUM_ROWS)
        cc = _dot(ll, la_hi[blk]) + _dot(ll, la_lo[blk])
        cums.append(cc[:CUM_ROWS])
        cls.append(cc[CUM_ROWS:])
    cum = jnp.concatenate(cums, axis=0)
    cl = jnp.concatenate(cls, axis=0)
    e_pos = jnp.exp(cum)
    e_neg = jnp.exp(-cum)
    qs = gq * (DK ** -0.5)

    cos = cos_ref[...]
    sin = sin_ref[...]

    def rope(t):
        t1 = t[:, :LANES]
        t2 = t[:, LANES:]
        return jnp.concatenate([t1 * cos - t2 * sin, t2 * cos + t1 * sin], axis=1)

    qr = rope(rq) * (DK ** -0.5)
    kr = rope(rk)

    h1_s[...] = h1
    qf_s[...] = (qs * e_pos).astype(BF16)
    qb_s[...] = (qs * e_neg).astype(BF16)
    _store_chunks_transposed(kf_s, gk * e_neg)
    _store_chunks_transposed(kb_s, gk * e_pos)
    _store_chunks_transposed(ks_s, gk * jnp.exp(cl - cum))
    _store_chunks_transposed(dec_s, jnp.exp(cl))
    qr_s[...] = qr.astype(BF16)
    qd_s[...] = (qr * qdec_ref[...]).astype(BF16)
    _store_chunks_transposed(kr_s, kr)
    _store_chunks_transposed(kd_s, kr * kdec_ref[...])

    gvr = _dot(n, wgvr_ref[...])
    gv_s[...] = gvr[:, :VW].astype(BF16)
    gr_s[...] = gvr[:, VW:]
    rvg = _dot(n, wrvg_ref[...])
    rv_s[...] = rvg[:, :VW].astype(BF16)
    rg_s[...] = rvg[:, VW:]


def _call_b_kernel(h_ref, p_ref, g2_ref, wg_ref, wu_ref, wd_ref, gple_ref, wpg_ref, wpp_ref, gfin_ref,
                   out_ref, *, final):
    h = h_ref[0]
    f = _swiglu(_rms(h, g2_ref[...]).astype(BF16), wg_ref, wu_ref, wd_ref)
    h3 = h + 0.5 * f
    gate = _sigmoid(_dot(_rms(h3, gple_ref[...]).astype(BF16), wpg_ref[...]))
    h4 = h3 + gate * _dot(p_ref[0].astype(BF16), wpp_ref[...])
    out_ref[0] = _rms(h4, gfin_ref[...]) if final else h4


def _const_spec(shape):
    return pl.BlockSpec(shape, lambda *_: (0,) * len(shape), pipeline_mode=pl.Buffered(1))


def _mixer_constants(tile):
    r = np.arange(QK)
    kcol_g = r // DK
    kcol_r = (r % LANES) // (DK // 2)
    row_h = r // CHUNK
    mkg = (row_h[:, None] == kcol_g[None, :])
    mkr = (row_h[:, None] == kcol_r[None, :])
    mv = (row_h[:, None] == (np.arange(VW) // DV)[None, :])
    vrow_h = np.arange(VW) // DV
    mwg = (vrow_h[:, None] == kcol_g[None, :])
    mwr = (vrow_h[:, None] == kcol_r[None, :])
    i = np.arange(CHUNK)
    j = r % CHUNK
    causal = (i[:, None] >= j[None, :]).astype(np.float32)
    t = np.arange(CUM_ROWS)
    ll = np.concatenate([
        (t[:, None] // CHUNK == t[None, :] // CHUNK) & (t[None, :] <= t[:, None]),
        (t[:, None] // CHUNK == t[None, :] // CHUNK)], axis=0)

    log_gamma = np.log(1.0 - 2.0 ** (-5.0 - np.arange(HEADS, dtype=np.float64)))
    pos = np.arange(CHUNK, dtype=np.float64)
    dist = np.abs(pos[:, None] - pos[None, :])
    dmask = np.exp(log_gamma[:, None, None] * dist)
    dmask = dmask.transpose(1, 0, 2).reshape(CHUNK, QK)
    lg_col = log_gamma[kcol_r]
    qdec = np.exp(lg_col[None, :] * (pos + 1.0)[:, None])
    kdec = np.exp(lg_col[None, :] * (CHUNK - 1 - pos)[:, None])
    rdecay = np.tile(np.exp(lg_col * CHUNK)[:, None], (1, DV))
    reps = tile // CHUNK
    as_bf = lambda m: jnp.asarray(m.astype(np.float32), dtype=BF16)
    as_f32 = lambda m: jnp.asarray(m.astype(np.float32))
    return dict(ll=as_bf(ll), mkg=as_bf(mkg.T), mkr=as_bf(mkr.T), mv=as_bf(mv), mwg=as_bf(mwg.T), mwr=as_bf(mwr.T),
                causal=as_f32(causal), dmask=as_f32(dmask), qdec=as_f32(np.tile(qdec, (reps, 1))),
                kdec=as_f32(np.tile(kdec, (reps, 1))), rdecay=as_f32(rdecay))


def _rope_tables(seq):
    half = DK // 2
    inv = ROPE_BASE ** (-np.arange(half, dtype=np.float64) / half)
    ang = np.arange(seq, dtype=np.float64)[:, None] * inv[None, :]
    table = lambda t: jnp.asarray(np.tile(t, (1, HEADS)).astype(np.float32))
    return table(np.cos(ang)), table(np.sin(ang))


def _rope_halves_first(w):
    rows = w.shape[0]
    return w.reshape(rows, HEADS, 2, DK // 2).transpose(0, 2, 1, 3).reshape(rows, QK)


def _call_a_with_casts_kernel(*refs, n_in, n_cast, **kw):
    ins, cast_in = refs[:n_in], refs[n_in:n_in + n_cast]
    out_ref, cast_out = refs[n_in + n_cast], refs[n_in + n_cast + 1:n_in + 2 * n_cast + 1]
    scratch = refs[n_in + 2 * n_cast + 1:]
    for src, dst in zip(cast_in, cast_out):
        dst[...] = src[...].astype(BF16)
    step, last = pl.program_id(0), pl.num_programs(0) - 1
    pl.when(step < last)(functools.partial(_call_a_kernel, *ins, out_ref, *scratch, flush=False, **kw))
    pl.when(step == last)(functools.partial(_call_a_kernel, *ins, out_ref, *scratch, flush=True, **kw))


def _cast_block_rows(rows, max_steps):
    for block in range(BF16_SUBLANES, rows + 1, BF16_SUBLANES):
        if rows % block == 0 and rows // block <= max_steps:
            return block
    raise ValueError(f"no row block for {rows} rows in {max_steps} steps")


def _call_a(h, cos, sin, consts, g1, wg, wu, wd, gmix, win_parts, wup, bgate, gout, wout, cast_weights):
    b, s, d = h.shape
    tile = TILE_A
    tps = s // tile
    n_tiles = b * tps
    row = lambda a: a.reshape(1, -1)
    args = [h, cos, sin, row(g1), wg, wu, wd, row(gmix), *win_parts, wup, row(bgate), row(gout), wout,
            consts["ll"], consts["mkg"], consts["mkr"], consts["mv"], consts["mwg"], consts["mwr"],
            consts["causal"], consts["dmask"], consts["qdec"], consts["kdec"], consts["rdecay"]]
    proj_tile = lambda st: jnp.minimum(st, n_tiles - 1)
    mix_tile = lambda st: jnp.maximum(st - 1, 0)
    in_specs = [pl.BlockSpec((1, tile, d), lambda st: (proj_tile(st) // tps, proj_tile(st) % tps, 0)),
                pl.BlockSpec((tile, LANES), lambda st: (proj_tile(st) % tps, 0)),
                pl.BlockSpec((tile, LANES), lambda st: (proj_tile(st) % tps, 0))]
    in_specs += [_const_spec(a.shape) for a in args[3:]]
    bf = lambda *shape: pltpu.VMEM(shape, BF16)
    f32 = lambda *shape: pltpu.VMEM(shape, F32)
    kt = bf(tile // CHUNK * QK, 2 * CHUNK)
    scratch = [f32(QK, DV), f32(QK, DV), f32(tile, 2 * VW), f32(tile, d), f32(tile, VW), f32(tile, VW),
               bf(tile, QK), bf(tile, QK), kt, kt, kt, bf(tile, VW),
               f32(tile // CHUNK * QK, 2 * CHUNK),
               bf(tile, QK), bf(tile, QK), kt, kt, bf(tile, VW)]
    out_specs = [pl.BlockSpec((1, tile, d), lambda st: (mix_tile(st) // tps, mix_tile(st) % tps, 0))]
    out_shape = [jax.ShapeDtypeStruct((b, s, d), F32)]
    n_in = len(args)
    for w in cast_weights:
        rows, cols = w.shape
        block = _cast_block_rows(rows, n_tiles)
        spec = pl.BlockSpec((block, cols), functools.partial(lambda st, last: (jnp.minimum(st, last), 0),
                                                             last=rows // block - 1))
        in_specs.append(spec)
        out_specs.append(spec)
        out_shape.append(jax.ShapeDtypeStruct(w.shape, BF16))
    outs = pl.pallas_call(
        functools.partial(_call_a_with_casts_kernel, n_in=n_in, n_cast=len(cast_weights),
                          tile=tile, tiles_per_seq=tps),
        grid=(n_tiles + 1,),
        in_specs=in_specs,
        out_specs=out_specs,
        out_shape=out_shape,
        scratch_shapes=scratch,
        compiler_params=pltpu.CompilerParams(dimension_semantics=("arbitrary",),
                                             vmem_limit_bytes=VMEM_LIMIT),
        name="ffn_a_mixers",
    )(*args, *cast_weights)
    return outs[0], outs[1:]


def _call_b(h, p, g2, wg, wu, wd, gple, wpg, wpp, gfin, final):
    b, s, d = h.shape
    tile = TILE_B
    row = lambda a: a.reshape(1, -1)
    args = [h, p, row(g2), wg, wu, wd, row(gple), wpg, wpp, row(gfin)]
    in_specs = [pl.BlockSpec((1, tile, d), lambda bi, si: (bi, si, 0)),
                pl.BlockSpec((1, tile, p.shape[-1]), lambda bi, si: (bi, si, 0))]
    in_specs += [_const_spec(a.shape) for a in args[2:]]
    return pl.pallas_call(
        functools.partial(_call_b_kernel, final=final),
        grid=(b, s // tile),
        in_specs=in_specs,
        out_specs=pl.BlockSpec((1, tile, d), lambda bi, si: (bi, si, 0)),
        out_shape=jax.ShapeDtypeStruct((b, s, d), F32),
        compiler_params=pltpu.CompilerParams(dimension_semantics=("arbitrary", "arbitrary"),
                                             vmem_limit_bytes=VMEM_LIMIT),
        name="ffn_b_embed",
    )(*args)


def kernel(x, p, g_ffn1, w_ffn1_gate, w_ffn1_up, w_ffn1_down, g_mix, w_in, w_gla_gate_up, b_gla_gate, g_gla_out, g_ret_out, w_out, g_ffn2, w_ffn2_gate, w_ffn2_up, w_ffn2_down, g_ple, w_ple_gate, w_ple_proj, g_final):
    depth = p.shape[0]
    seq = x.shape[1]
    assert seq % TILE_A == 0 and seq % TILE_B == 0 and TILE_A % CHUNK == 0
    assert w_ffn1_gate.shape[2] % MXU_COLS == 0
    consts = _mixer_constants(TILE_A)
    cos, sin = _rope_tables(seq)
    sizes = [QK, QK, VW, VW, GATE_RANK, QK, QK, VW, VW]
    offs = np.concatenate([[0], np.cumsum(sizes)])
    bf = lambda a: a.astype(BF16)
    h = x
    for i in range(depth):
        wi = w_in[i]
        cols = lambda j0, j1: bf(wi[:, offs[j0]:offs[j1]])
        win_parts = [jnp.pad(cols(4, 5), ((0, 0), (0, GATE_RANK_PAD - GATE_RANK))),
                     cols(0, 2),
                     _rope_halves_first(cols(5, 6)), _rope_halves_first(cols(6, 7)),
                     cols(2, 4), cols(7, 9)]
        wup = bf(jnp.pad(w_gla_gate_up[i], ((0, GATE_RANK_PAD - GATE_RANK), (0, 0))))
        gout = jnp.concatenate([g_gla_out[i], g_ret_out[i]])
        next_weights = [w_ffn2_gate[i], w_ffn2_up[i], w_ffn2_down[i], w_ple_gate[i], w_ple_proj[i]]
        h, (wg2, wu2, wd2, wpg, wpp) = _call_a(
            h, cos, sin, consts, g_ffn1[i], bf(w_ffn1_gate[i]), bf(w_ffn1_up[i]), bf(w_ffn1_down[i]),
            g_mix[i], win_parts, wup, b_gla_gate[i], gout, bf(w_out[i]), next_weights)
        h = _call_b(h, p[i], g_ffn2[i], wg2, wu2, wd2, g_ple[i], wpg, wpp, g_final, final=(i == depth - 1))
    return h
```

```python
import functools

import jax
import jax.numpy as jnp
import numpy as np
from jax import lax
from jax.experimental import pallas as pl
from jax.experimental.pallas import tpu as pltpu

CHUNK = 64
HEADS = 4
DK = 64
DV = 128
QK = HEADS * DK
VW = HEADS * DV
GATE_RANK = 16
GATE_RANK_PAD = 128
GATE_NORM = 16.0
ROPE_BASE = 10000.0
EPS = 1e-6
LANES = 128
BF16_SUBLANES = 16
MXU_COLS = 256

TILE_A = 512
CUM_ROWS = 256
TILE_B = 512
TAIL_GROUPS_B = 2
VMEM_LIMIT = 60 * 1024 * 1024

F32 = jnp.float32
BF16 = jnp.bfloat16


def _dot(a, b):
    return jnp.dot(a, b, preferred_element_type=F32)


def _dot_nt(a, b):
    return lax.dot_general(a, b, (((1,), (1,)), ((), ())), preferred_element_type=F32)


def _dot_tn(a, b):
    return lax.dot_general(a, b, (((0,), (0,)), ((), ())), preferred_element_type=F32)


def _rms(x, g):
    return x * lax.rsqrt(jnp.mean(x * x, axis=-1, keepdims=True) + EPS) * g


def _sigmoid(x):
    return 1.0 / (1.0 + jnp.exp(-x))


def _silu(x):
    return x * _sigmoid(x)


def _log_sigmoid(x):
    return jnp.minimum(x, 0.0) - jnp.log1p(jnp.exp(-jnp.abs(x)))


def _swiglu(xn, wg_ref, wu_ref, wd_ref):
    g = _dot(xn, wg_ref[...])
    u = _dot(xn, wu_ref[...])
    return _dot((_silu(g) * u).astype(BF16), wd_ref[...])


def _swiglu_up(xn, wg_ref, wu_ref, j):
    cols = slice(j * MXU_COLS, (j + 1) * MXU_COLS)
    return _dot(xn, wg_ref[:, cols]), _dot(xn, wu_ref[:, cols])


def _swiglu_down(gu, wd_ref, j):
    g, u = gu
    return _dot((_silu(g) * u).astype(BF16), wd_ref[j * MXU_COLS:(j + 1) * MXU_COLS, :])


def _tile4(a):
    return jnp.concatenate([a, a, a, a], axis=0)


def _lanes2(a):
    return jnp.concatenate([a, a], axis=1)


def _lanes4(a):
    return jnp.concatenate([a, a, a, a], axis=1)


def _store_chunks_transposed(dst_ref, k):
    for c in range(k.shape[0] // CHUNK):
        blk = k[c * CHUNK:(c + 1) * CHUNK, :]
        dst_ref[c * QK:(c + 1) * QK, :] = jnp.concatenate([blk, blk], axis=0).T.astype(dst_ref.dtype)


def _call_a_kernel(x_ref, cos_ref, sin_ref, g1_ref, wg_ref, wu_ref, wd_ref, gmix_ref,
                   wlr_ref, wgqk_ref, wrq_ref, wrk_ref, wgvr_ref, wrvg_ref,
                   wup_ref, bgate_ref, gout_ref, wout_ref, ll_ref,
                   mkg_ref, mkr_ref, mv_ref, mwg_ref, mwr_ref, causal_ref, dmask_ref,
                   qdec_ref, kdec_ref, rdecay_ref,
                   out_ref,
                   st_g, st_r, o_s, h1_s, gr_s, rg_s, qf_s, qb_s, kf_s, kb_s, ks_s, gv_s, dec_s,
                   qr_s, qd_s, kr_s, kd_s, rv_s, *, tile, tiles_per_seq, flush, side_job):
    step = pl.program_id(0)
    carried = (o_s, h1_s, gr_s, rg_s, qf_s, qb_s, kf_s, kb_s, ks_s, gv_s, dec_s, qr_s, qd_s, kr_s, kd_s, rv_s)

    @pl.when(step == 0)
    def _():
        for ref in (st_g, st_r) + carried:
            ref[...] = jnp.zeros_like(ref)

    keep = lax.rem(jnp.maximum(step - 1, 0), tiles_per_seq) != 0
    st_g[...] = jnp.where(keep, st_g[...], 0.0)
    st_r[...] = jnp.where(keep, st_r[...], 0.0)

    half = DK // 2

    def gla_chunk(c):
        rows = pl.ds(c * CHUNK, CHUNK)
        qf = qf_s[rows, :]
        krows = pl.ds(c * QK, QK)
        s_f = _dot(qf, _lanes2(kf_s[krows, :]) * mkg_ref[...])
        s_b = _dot(qb_s[rows, :], _lanes2(kb_s[krows, :]) * mkg_ref[...])
        p = jnp.where(causal_ref[...] > 0.5, s_f, s_b).astype(BF16)
        v = gv_s[rows, :]
        st = st_g[...]
        w = _lanes4(st.astype(BF16)) * mwg_ref[...]
        o_s[rows, 0:VW] = _dot(p, _tile4(v) * mv_ref[...]) + _dot(qf, w)
        kt = ks_s[krows, :][:, :CHUNK]
        upd = [_dot(kt[h * DK:(h + 1) * DK, :], v[:, h * DV:(h + 1) * DV]) for h in range(HEADS)]
        st_g[...] = st * dec_s[krows, :] + jnp.concatenate(upd, axis=0)

    def ret_chunk(c):
        rows = pl.ds(c * CHUNK, CHUNK)
        s = _dot(qr_s[rows, :], _lanes2(kr_s[pl.ds(c * QK, QK), :]) * mkr_ref[...]) * dmask_ref[...]
        v = rv_s[rows, :]
        st = st_r[...]
        w = _lanes4(st.astype(BF16)) * mwr_ref[...]
        o_s[rows, VW:2 * VW] = _dot(s.astype(BF16), _tile4(v) * mv_ref[...]) + _dot(qd_s[rows, :], w)
        kt = kd_s[pl.ds(c * QK, QK), :][:, :CHUNK]
        upd = []
        for h in range(HEADS):
            k_h = jnp.concatenate([kt[h * half:(h + 1) * half, :],
                                   kt[LANES + h * half:LANES + (h + 1) * half, :]], axis=0)
            upd.append(_dot(k_h, v[:, h * DV:(h + 1) * DV]))
        upd = [u[:half] for u in upd] + [u[half:] for u in upd]
        st_r[...] = st * rdecay_ref[...] + jnp.concatenate(upd, axis=0)

    def mix_operand():
        gout = gout_ref[...]
        gates = (gr_s, rg_s)
        parts = []
        for h in range(2 * HEADS):
            oh = o_s[:, h * DV:(h + 1) * DV]
            gate = gates[h // HEADS][:, (h % HEADS) * DV:(h % HEADS + 1) * DV]
            parts.append((_rms(oh, gout[:, h * DV:(h + 1) * DV]) * _silu(gate)).astype(BF16))
        return jnp.concatenate(parts, axis=1)

    mixer_units = []
    for c in range(tile // CHUNK):
        mixer_units += [functools.partial(gla_chunk, c), functools.partial(ret_chunk, c)]

    if flush:
        side_job()
        for unit in mixer_units:
            unit()
        out_ref[0] = h1_s[...] + _dot(mix_operand(), wout_ref[...])
        return

    mixer_units[0]()
    x = x_ref[0]
    xn = _rms(x, g1_ref[...]).astype(BF16)
    n_pieces = wd_ref.shape[0] // MXU_COLS
    spread = max(n_pieces - 2, 1)
    n_rest = len(mixer_units) - 1
    f = None
    mix = None
    gu = _swiglu_up(xn, wg_ref, wu_ref, 0)
    for j in range(n_pieces):
        gu_next = _swiglu_up(xn, wg_ref, wu_ref, j + 1) if j + 1 < n_pieces else None
        fj = _swiglu_down(gu, wd_ref, j)
        f = fj if f is None else f + fj
        gu = gu_next
        if j < spread:
            for u in range(1 + (j * n_rest) // spread, 1 + ((j + 1) * n_rest) // spread):
                mixer_units[u]()
        if j == spread - 1:
            mix = mix_operand()
        if j == n_pieces // 2:
            side_job()
    out_ref[0] = h1_s[...] + _dot(mix, wout_ref[...])
    h1 = x + 0.5 * f
    n = _rms(h1, gmix_ref[...]).astype(BF16)

    glr = _dot(n, wlr_ref[...])
    gqk = _dot(n, wgqk_ref[...])
    rq = _dot(n, wrq_ref[...])
    rk = _dot(n, wrk_ref[...])
    gvr = _dot(n, wgvr_ref[...])
    gv_s[...] = gvr[:, :VW].astype(BF16)
    gr_s[...] = gvr[:, VW:]
    gq = gqk[:, :QK]
    gk = gqk[:, QK:]

    logit = _dot(glr.astype(BF16), wup_ref[...]) + bgate_ref[...]
    log_a = _log_sigmoid(logit) * (1.0 / GATE_NORM)
    la_hi = log_a.astype(BF16)
    la_lo = (log_a - la_hi.astype(F32)).astype(BF16)
    ll = ll_ref[...]
    cums, cls = [], []
    for r in range(tile // CUM_ROWS):
        blk = slice(r * CUM_ROWS, (r + 1) * CUM_ROWS)
        cc = _dot(ll, la_hi[blk]) + _dot(ll, la_lo[blk])
        cums.append(cc[:CUM_ROWS])
        cls.append(cc[CUM_ROWS:])
    cum = jnp.concatenate(cums, axis=0)
    cl = jnp.concatenate(cls, axis=0)
    e_pos = jnp.exp(cum)
    e_neg = jnp.exp(-cum)
    qs = gq * (DK ** -0.5)

    cos = cos_ref[...]
    sin = sin_ref[...]

    def rope(t):
        t1 = t[:, :LANES]
        t2 = t[:, LANES:]
        return jnp.concatenate([t1 * cos - t2 * sin, t2 * cos + t1 * sin], axis=1)

    qr = rope(rq) * (DK ** -0.5)
    kr = rope(rk)

    h1_s[...] = h1
    qf_s[...] = (qs * e_pos).astype(BF16)
    qb_s[...] = (qs * e_neg).astype(BF16)
    _store_chunks_transposed(kf_s, gk * e_neg)
    _store_chunks_transposed(kb_s, gk * e_pos)
    _store_chunks_transposed(ks_s, gk * jnp.exp(cl - cum))
    _store_chunks_transposed(dec_s, jnp.exp(cl))
    qr_s[...] = qr.astype(BF16)
    qd_s[...] = (qr * qdec_ref[...]).astype(BF16)
    _store_chunks_transposed(kr_s, kr)
    _store_chunks_transposed(kd_s, kr * kdec_ref[...])

    rvg = _dot(n, wrvg_ref[...])
    rv_s[...] = rvg[:, :VW].astype(BF16)
    rg_s[...] = rvg[:, VW:]


def _call_b_kernel(h_ref, p_ref, g2_ref, wg_ref, wu_ref, wd_ref, gple_ref, wpg_ref, wpp_ref, gfin_ref,
                   out_ref, *, final):
    h = h_ref[0]
    f = _swiglu(_rms(h, g2_ref[...]).astype(BF16), wg_ref, wu_ref, wd_ref)
    pe = _dot(p_ref[0].astype(BF16), wpp_ref[...])
    h3 = h + 0.5 * f
    n3 = _rms(h3, gple_ref[...]).astype(BF16)
    group_rows = h.shape[0] // TAIL_GROUPS_B
    groups = [slice(k * group_rows, (k + 1) * group_rows) for k in range(TAIL_GROUPS_B)]
    gates = [_dot(n3[rows], wpg_ref[...]) for rows in groups]
    for rows, gate in zip(groups, gates):
        h4 = h3[rows] + _sigmoid(gate) * pe[rows]
        out_ref[0, rows, :] = _rms(h4, gfin_ref[...]) if final else h4


def _const_spec(shape):
    return pl.BlockSpec(shape, lambda *_: (0,) * len(shape), pipeline_mode=pl.Buffered(1))


def _mixer_constants(tile):
    r = np.arange(QK)
    kcol_g = r // DK
    kcol_r = (r % LANES) // (DK // 2)
    row_h = r // CHUNK
    mkg = (row_h[:, None] == kcol_g[None, :])
    mkr = (row_h[:, None] == kcol_r[None, :])
    mv = (row_h[:, None] == (np.arange(VW) // DV)[None, :])
    vrow_h = np.arange(VW) // DV
    mwg = (vrow_h[:, None] == kcol_g[None, :])
    mwr = (vrow_h[:, None] == kcol_r[None, :])
    i = np.arange(CHUNK)
    j = r % CHUNK
    causal = (i[:, None] >= j[None, :]).astype(np.float32)
    t = np.arange(CUM_ROWS)
    ll = np.concatenate([
        (t[:, None] // CHUNK == t[None, :] // CHUNK) & (t[None, :] <= t[:, None]),
        (t[:, None] // CHUNK == t[None, :] // CHUNK)], axis=0)

    log_gamma = np.log(1.0 - 2.0 ** (-5.0 - np.arange(HEADS, dtype=np.float64)))
    pos = np.arange(CHUNK, dtype=np.float64)
    dist = np.abs(pos[:, None] - pos[None, :])
    dmask = np.exp(log_gamma[:, None, None] * dist)
    dmask = dmask.transpose(1, 0, 2).reshape(CHUNK, QK)
    lg_col = log_gamma[kcol_r]
    qdec = np.exp(lg_col[None, :] * (pos + 1.0)[:, None])
    kdec = np.exp(lg_col[None, :] * (CHUNK - 1 - pos)[:, None])
    rdecay = np.tile(np.exp(lg_col * CHUNK)[:, None], (1, DV))
    reps = tile // CHUNK
    as_bf = lambda m: jnp.asarray(m.astype(np.float32), dtype=BF16)
    as_f32 = lambda m: jnp.asarray(m.astype(np.float32))
    return dict(ll=as_bf(ll), mkg=as_bf(mkg.T), mkr=as_bf(mkr.T), mv=as_bf(mv), mwg=as_bf(mwg.T), mwr=as_bf(mwr.T),
                causal=as_f32(causal), dmask=as_f32(dmask), qdec=as_f32(np.tile(qdec, (reps, 1))),
                kdec=as_f32(np.tile(kdec, (reps, 1))), rdecay=as_f32(rdecay))


def _rope_tables(seq):
    half = DK // 2
    inv = ROPE_BASE ** (-np.arange(half, dtype=np.float64) / half)
    ang = np.arange(seq, dtype=np.float64)[:, None] * inv[None, :]
    table = lambda t: jnp.asarray(np.tile(t, (1, HEADS)).astype(np.float32))
    return table(np.cos(ang)), table(np.sin(ang))


def _rope_halves_first(w):
    rows = w.shape[0]
    return w.reshape(rows, HEADS, 2, DK // 2).transpose(0, 2, 1, 3).reshape(rows, QK)


def _call_a_with_casts_kernel(*refs, n_in, n_cast, **kw):
    ins, cast_in = refs[:n_in], refs[n_in:n_in + n_cast]
    out_ref, cast_out = refs[n_in + n_cast], refs[n_in + n_cast + 1:n_in + 2 * n_cast + 1]
    scratch = refs[n_in + 2 * n_cast + 1:]
    def side_job():
        for src, dst in zip(cast_in, cast_out):
            dst[...] = src[...].astype(BF16)

    step, last = pl.program_id(0), pl.num_programs(0) - 1
    body = functools.partial(_call_a_kernel, *ins, out_ref, *scratch, side_job=side_job, **kw)
    pl.when(step < last)(functools.partial(body, flush=False))
    pl.when(step == last)(functools.partial(body, flush=True))


def _cast_block_rows(rows, max_steps):
    for block in range(BF16_SUBLANES, rows + 1, BF16_SUBLANES):
        if rows % block == 0 and rows // block <= max_steps:
            return block
    raise ValueError(f"no row block for {rows} rows in {max_steps} steps")


def _call_a(h, cos, sin, consts, g1, wg, wu, wd, gmix, win_parts, wup, bgate, gout, wout, cast_weights):
    b, s, d = h.shape
    tile = TILE_A
    tps = s // tile
    n_tiles = b * tps
    row = lambda a: a.reshape(1, -1)
    args = [h, cos, sin, row(g1), wg, wu, wd, row(gmix), *win_parts, wup, row(bgate), row(gout), wout,
            consts["ll"], consts["mkg"], consts["mkr"], consts["mv"], consts["mwg"], consts["mwr"],
            consts["causal"], consts["dmask"], consts["qdec"], consts["kdec"], consts["rdecay"]]
    proj_tile = lambda st: jnp.minimum(st, n_tiles - 1)
    mix_tile = lambda st: jnp.maximum(st - 1, 0)
    in_specs = [pl.BlockSpec((1, tile, d), lambda st: (proj_tile(st) // tps, proj_tile(st) % tps, 0)),
                pl.BlockSpec((tile, LANES), lambda st: (proj_tile(st) % tps, 0)),
                pl.BlockSpec((tile, LANES), lambda st: (proj_tile(st) % tps, 0))]
    in_specs += [_const_spec(a.shape) for a in args[3:]]
    bf = lambda *shape: pltpu.VMEM(shape, BF16)
    f32 = lambda *shape: pltpu.VMEM(shape, F32)
    kt = bf(tile // CHUNK * QK, 2 * CHUNK)
    scratch = [f32(QK, DV), f32(QK, DV), f32(tile, 2 * VW), f32(tile, d), f32(tile, VW), f32(tile, VW),
               bf(tile, QK), bf(tile, QK), kt, kt, kt, bf(tile, VW),
               f32(tile // CHUNK * QK, 2 * CHUNK),
               bf(tile, QK), bf(tile, QK), kt, kt, bf(tile, VW)]
    out_specs = [pl.BlockSpec((1, tile, d), lambda st: (mix_tile(st) // tps, mix_tile(st) % tps, 0))]
    out_shape = [jax.ShapeDtypeStruct((b, s, d), F32)]
    n_in = len(args)
    for w in cast_weights:
        rows, cols = w.shape
        block = _cast_block_rows(rows, n_tiles)
        spec = pl.BlockSpec((block, cols), functools.partial(lambda st, last: (jnp.minimum(st, last), 0),
                                                             last=rows // block - 1))
        in_specs.append(spec)
        out_specs.append(spec)
        out_shape.append(jax.ShapeDtypeStruct(w.shape, BF16))
    outs = pl.pallas_call(
        functools.partial(_call_a_with_casts_kernel, n_in=n_in, n_cast=len(cast_weights),
                          tile=tile, tiles_per_seq=tps),
        grid=(n_tiles + 1,),
        in_specs=in_specs,
        out_specs=out_specs,
        out_shape=out_shape,
        scratch_shapes=scratch,
        compiler_params=pltpu.CompilerParams(dimension_semantics=("arbitrary",),
                                             vmem_limit_bytes=VMEM_LIMIT),
        name="ffn_a_mixers",
    )(*args, *cast_weights)
    return outs[0], outs[1:]


def _call_b(h, p, g2, wg, wu, wd, gple, wpg, wpp, gfin, final):
    b, s, d = h.shape
    tile = TILE_B
    row = lambda a: a.reshape(1, -1)
    args = [h, p, row(g2), wg, wu, wd, row(gple), wpg, wpp, row(gfin)]
    in_specs = [pl.BlockSpec((1, tile, d), lambda bi, si: (bi, si, 0)),
                pl.BlockSpec((1, tile, p.shape[-1]), lambda bi, si: (bi, si, 0))]
    in_specs += [_const_spec(a.shape) for a in args[2:]]
    return pl.pallas_call(
        functools.partial(_call_b_kernel, final=final),
        grid=(b, s // tile),
        in_specs=in_specs,
        out_specs=pl.BlockSpec((1, tile, d), lambda bi, si: (bi, si, 0)),
        out_shape=jax.ShapeDtypeStruct((b, s, d), F32),
        compiler_params=pltpu.CompilerParams(dimension_semantics=("arbitrary", "arbitrary"),
                                             vmem_limit_bytes=VMEM_LIMIT),
        name="ffn_b_embed",
    )(*args)


def kernel(x, p, g_ffn1, w_ffn1_gate, w_ffn1_up, w_ffn1_down, g_mix, w_in, w_gla_gate_up, b_gla_gate, g_gla_out, g_ret_out, w_out, g_ffn2, w_ffn2_gate, w_ffn2_up, w_ffn2_down, g_ple, w_ple_gate, w_ple_proj, g_final):
    depth = p.shape[0]
    seq = x.shape[1]
    assert seq % TILE_A == 0 and seq % TILE_B == 0 and TILE_A % CHUNK == 0
    assert w_ffn1_gate.shape[2] % MXU_COLS == 0
    consts = _mixer_constants(TILE_A)
    cos, sin = _rope_tables(seq)
    sizes = [QK, QK, VW, VW, GATE_RANK, QK, QK, VW, VW]
    offs = np.concatenate([[0], np.cumsum(sizes)])
    bf = lambda a: a.astype(BF16)
    h = x
    for i in range(depth):
        wi = w_in[i]
        cols = lambda j0, j1: bf(wi[:, offs[j0]:offs[j1]])
        win_parts = [jnp.pad(cols(4, 5), ((0, 0), (0, GATE_RANK_PAD - GATE_RANK))),
                     cols(0, 2),
                     _rope_halves_first(cols(5, 6)), _rope_halves_first(cols(6, 7)),
                     cols(2, 4), cols(7, 9)]
        wup = bf(jnp.pad(w_gla_gate_up[i], ((0, GATE_RANK_PAD - GATE_RANK), (0, 0))))
        gout = jnp.concatenate([g_gla_out[i], g_ret_out[i]])
        next_weights = [w_ffn2_gate[i], w_ffn2_up[i], w_ffn2_down[i], w_ple_gate[i], w_ple_proj[i]]
        h, (wg2, wu2, wd2, wpg, wpp) = _call_a(
            h, cos, sin, consts, g_ffn1[i], bf(w_ffn1_gate[i]), bf(w_ffn1_up[i]), bf(w_ffn1_down[i]),
            g_mix[i], win_parts, wup, b_gla_gate[i], gout, bf(w_out[i]), next_weights)
        h = _call_b(h, p[i], g_ffn2[i], wg2, wu2, wd2, g_ple[i], wpg, wpp, g_final, final=(i == depth - 1))
    return h
```

```python
import functools

import jax
import jax.numpy as jnp
import numpy as np
from jax import lax
from jax.experimental import pallas as pl
from jax.experimental.pallas import tpu as pltpu

CHUNK = 64
HEADS = 4
DK = 64
DV = 128
QK = HEADS * DK
VW = HEADS * DV
GATE_RANK = 16
GATE_RANK_PAD = 128
GATE_NORM = 16.0
ROPE_BASE = 10000.0
EPS = 1e-6
LANES = 128
BF16_SUBLANES = 16
MXU_COLS = 256

TILE_A = 512
CUM_ROWS = 256
TILE_B = 512
TAIL_GROUPS_B = 2
VMEM_LIMIT = 60 * 1024 * 1024

F32 = jnp.float32
BF16 = jnp.bfloat16


def _dot(a, b):
    return jnp.dot(a, b, preferred_element_type=F32)


def _dot_nt(a, b):
    return lax.dot_general(a, b, (((1,), (1,)), ((), ())), preferred_element_type=F32)


def _dot_tn(a, b):
    return lax.dot_general(a, b, (((0,), (0,)), ((), ())), preferred_element_type=F32)


def _rms(x, g):
    return x * lax.rsqrt(jnp.mean(x * x, axis=-1, keepdims=True) + EPS) * g


def _sigmoid(x):
    return 1.0 / (1.0 + jnp.exp(-x))


def _silu(x):
    return x * _sigmoid(x)


def _log_sigmoid(x):
    return jnp.minimum(x, 0.0) - jnp.log1p(jnp.exp(-jnp.abs(x)))


def _swiglu(xn, wg_ref, wu_ref, wd_ref):
    g = _dot(xn, wg_ref[...])
    u = _dot(xn, wu_ref[...])
    return _dot((_silu(g) * u).astype(BF16), wd_ref[...])


def _swiglu_up(xn, wg_ref, wu_ref, j):
    cols = slice(j * MXU_COLS, (j + 1) * MXU_COLS)
    return _dot(xn, wg_ref[:, cols]), _dot(xn, wu_ref[:, cols])


def _swiglu_down(gu, wd_ref, j):
    g, u = gu
    return _dot((_silu(g) * u).astype(BF16), wd_ref[j * MXU_COLS:(j + 1) * MXU_COLS, :])


def _tile4(a):
    return jnp.concatenate([a, a, a, a], axis=0)


def _lanes2(a):
    return jnp.concatenate([a, a], axis=1)


def _lanes4(a):
    return jnp.concatenate([a, a, a, a], axis=1)


def _store_chunks_transposed(dst_ref, k):
    for c in range(k.shape[0] // CHUNK):
        blk = k[c * CHUNK:(c + 1) * CHUNK, :]
        dst_ref[c * QK:(c + 1) * QK, :] = jnp.concatenate([blk, blk], axis=0).T.astype(dst_ref.dtype)


def _call_a_kernel(x_ref, cos_ref, sin_ref, g1_ref, wg_ref, wu_ref, wd_ref, gmix_ref,
                   wlr_ref, wgqk_ref, wrq_ref, wrk_ref, wgvr_ref, wrvg_ref,
                   wup_ref, bgate_ref, gout_ref, wout_ref, ll_ref,
                   mkg_ref, mkr_ref, mv_ref, mwg_ref, mwr_ref, causal_ref, dmask_ref,
                   qdec_ref, kdec_ref, rdecay_ref,
                   out_ref,
                   st_g, st_r, o_s, h1_s, gr_s, rg_s, qf_s, qb_s, kf_s, kb_s, ks_s, gv_s, dec_s,
                   qr_s, qd_s, kr_s, kd_s, rv_s, *, tile, tiles_per_seq, flush, side_job):
    step = pl.program_id(0)
    carried = (o_s, h1_s, gr_s, rg_s, qf_s, qb_s, kf_s, kb_s, ks_s, gv_s, dec_s, qr_s, qd_s, kr_s, kd_s, rv_s)

    @pl.when(step == 0)
    def _():
        for ref in (st_g, st_r) + carried:
            ref[...] = jnp.zeros_like(ref)

    keep = lax.rem(jnp.maximum(step - 1, 0), tiles_per_seq) != 0
    st_g[...] = jnp.where(keep, st_g[...], 0.0)
    st_r[...] = jnp.where(keep, st_r[...], 0.0)

    half = DK // 2

    def gla_chunk(c):
        rows = pl.ds(c * CHUNK, CHUNK)
        qf = qf_s[rows, :]
        krows = pl.ds(c * QK, QK)
        s_f = _dot(qf, _lanes2(kf_s[krows, :]) * mkg_ref[...])
        s_b = _dot(qb_s[rows, :], _lanes2(kb_s[krows, :]) * mkg_ref[...])
        p = jnp.where(causal_ref[...] > 0.5, s_f, s_b).astype(BF16)
        v = gv_s[rows, :]
        st = st_g[...]
        w = _lanes4(st.astype(BF16)) * mwg_ref[...]
        o_s[rows, 0:VW] = _dot(p, _tile4(v) * mv_ref[...]) + _dot(qf, w)
        kt = ks_s[krows, :][:, :CHUNK]
        upd = [_dot(kt[h * DK:(h + 1) * DK, :], v[:, h * DV:(h + 1) * DV]) for h in range(HEADS)]
        st_g[...] = st * dec_s[krows, :] + jnp.concatenate(upd, axis=0)

    def ret_chunk(c):
        rows = pl.ds(c * CHUNK, CHUNK)
        s = _dot(qr_s[rows, :], _lanes2(kr_s[pl.ds(c * QK, QK), :]) * mkr_ref[...]) * dmask_ref[...]
        v = rv_s[rows, :]
        st = st_r[...]
        w = _lanes4(st.astype(BF16)) * mwr_ref[...]
        o_s[rows, VW:2 * VW] = _dot(s.astype(BF16), _tile4(v) * mv_ref[...]) + _dot(qd_s[rows, :], w)
        kt = kd_s[pl.ds(c * QK, QK), :][:, :CHUNK]
        upd = []
        for h in range(HEADS):
            k_h = jnp.concatenate([kt[h * half:(h + 1) * half, :],
                                   kt[LANES + h * half:LANES + (h + 1) * half, :]], axis=0)
            upd.append(_dot(k_h, v[:, h * DV:(h + 1) * DV]))
        upd = [u[:half] for u in upd] + [u[half:] for u in upd]
        st_r[...] = st * rdecay_ref[...] + jnp.concatenate(upd, axis=0)

    def mix_operand():
        gout = gout_ref[...]
        gates = (gr_s, rg_s)
        parts = []
        for h in range(2 * HEADS):
            oh = o_s[:, h * DV:(h + 1) * DV]
            gate = gates[h // HEADS][:, (h % HEADS) * DV:(h % HEADS + 1) * DV]
            parts.append((_rms(oh, gout[:, h * DV:(h + 1) * DV]) * _silu(gate)).astype(BF16))
        return jnp.concatenate(parts, axis=1)

    mixer_units = []
    for c in range(tile // CHUNK):
        mixer_units += [functools.partial(gla_chunk, c), functools.partial(ret_chunk, c)]

    if flush:
        side_job()
        for unit in mixer_units:
            unit()
        out_ref[0] = h1_s[...] + _dot(mix_operand(), wout_ref[...])
        return

    mixer_units[0]()
    x = x_ref[0]
    xn = _rms(x, g1_ref[...]).astype(BF16)
    n_pieces = wd_ref.shape[0] // MXU_COLS
    spread = max(n_pieces - 2, 1)
    n_rest = len(mixer_units) - 1
    f = None
    mix = None
    gu = _swiglu_up(xn, wg_ref, wu_ref, 0)
    for j in range(n_pieces):
        gu_next = _swiglu_up(xn, wg_ref, wu_ref, j + 1) if j + 1 < n_pieces else None
        fj = _swiglu_down(gu, wd_ref, j)
        f = fj if f is None else f + fj
        gu = gu_next
        if j < spread:
            for u in range(1 + (j * n_rest) // spread, 1 + ((j + 1) * n_rest) // spread):
                mixer_units[u]()
        if j == spread - 1:
            mix = mix_operand()
        if j == n_pieces // 2:
            side_job()
    out_ref[0] = h1_s[...] + _dot(mix, wout_ref[...])
    h1 = x + 0.5 * f
    n = _rms(h1, gmix_ref[...]).astype(BF16)

    glr = _dot(n, wlr_ref[...])
    gqk = _dot(n, wgqk_ref[...])
    rq = _dot(n, wrq_ref[...])
    rk = _dot(n, wrk_ref[...])
    gq = gqk[:, :QK]
    gk = gqk[:, QK:]

    logit = _dot(glr.astype(BF16), wup_ref[...]) + bgate_ref[...]
    log_a = _log_sigmoid(logit) * (1.0 / GATE_NORM)
    la_hi = log_a.astype(BF16)
    la_lo = (log_a - la_hi.astype(F32)).astype(BF16)
    ll = ll_ref[...]
    cums, cls = [], []
    for r in range(tile // CUM_ROWS):
        blk = slice(r * CUM_ROWS, (r + 1) * CUM_ROWS)
        cc = _dot(ll, la_hi[blk]) + _dot(ll, la_lo[blk])
        cums.append(cc[:CUM_ROWS])
        cls.append(cc[CUM_ROWS:])
    cum = jnp.concatenate(cums, axis=0)
    cl = jnp.concatenate(cls, axis=0)
    e_pos = jnp.exp(cum)
    e_neg = jnp.exp(-cum)
    qs = gq * (DK ** -0.5)

    cos = cos_ref[...]
    sin = sin_ref[...]

    def rope(t):
        t1 = t[:, :LANES]
        t2 = t[:, LANES:]
        return jnp.concatenate([t1 * cos - t2 * sin, t2 * cos + t1 * sin], axis=1)

    qr = rope(rq) * (DK ** -0.5)
    kr = rope(rk)

    h1_s[...] = h1
    qf_s[...] = (qs * e_pos).astype(BF16)
    qb_s[...] = (qs * e_neg).astype(BF16)
    _store_chunks_transposed(kf_s, gk * e_neg)
    _store_chunks_transposed(kb_s, gk * e_pos)
    _store_chunks_transposed(ks_s, gk * jnp.exp(cl - cum))
    _store_chunks_transposed(dec_s, jnp.exp(cl))
    qr_s[...] = qr.astype(BF16)
    qd_s[...] = (qr * qdec_ref[...]).astype(BF16)
    _store_chunks_transposed(kr_s, kr)
    _store_chunks_transposed(kd_s, kr * kdec_ref[...])

    gvr = _dot(n, wgvr_ref[...])
    gv_s[...] = gvr[:, :VW].astype(BF16)
    gr_s[...] = gvr[:, VW:]
    rvg = _dot(n, wrvg_ref[...])
    rv_s[...] = rvg[:, :VW].astype(BF16)
    rg_s[...] = rvg[:, VW:]


def _call_b_kernel(h_ref, p_ref, g2_ref, wg_ref, wu_ref, wd_ref, gple_ref, wpg_ref, wpp_ref, gfin_ref,
                   out_ref, *, final):
    pe = _dot(p_ref[0].astype(BF16), wpp_ref[...])
    h = h_ref[0]
    f = _swiglu(_rms(h, g2_ref[...]).astype(BF16), wg_ref, wu_ref, wd_ref)
    group_rows = h.shape[0] // TAIL_GROUPS_B
    groups = [slice(k * group_rows, (k + 1) * group_rows) for k in range(TAIL_GROUPS_B)]
    h3s, gates = [], []
    for rows in groups:
        h3 = h[rows] + 0.5 * f[rows]
        h3s.append(h3)
        gates.append(_dot(_rms(h3, gple_ref[...]).astype(BF16), wpg_ref[...]))
    for rows, h3, gate in zip(groups, h3s, gates):
        h4 = h3 + _sigmoid(gate) * pe[rows]
        out_ref[0, rows, :] = _rms(h4, gfin_ref[...]) if final else h4


def _const_spec(shape):
    return pl.BlockSpec(shape, lambda *_: (0,) * len(shape), pipeline_mode=pl.Buffered(1))


def _mixer_constants(tile):
    r = np.arange(QK)
    kcol_g = r // DK
    kcol_r = (r % LANES) // (DK // 2)
    row_h = r // CHUNK
    mkg = (row_h[:, None] == kcol_g[None, :])
    mkr = (row_h[:, None] == kcol_r[None, :])
    mv = (row_h[:, None] == (np.arange(VW) // DV)[None, :])
    vrow_h = np.arange(VW) // DV
    mwg = (vrow_h[:, None] == kcol_g[None, :])
    mwr = (vrow_h[:, None] == kcol_r[None, :])
    i = np.arange(CHUNK)
    j = r % CHUNK
    causal = (i[:, None] >= j[None, :]).astype(np.float32)
    t = np.arange(CUM_ROWS)
    ll = np.concatenate([
        (t[:, None] // CHUNK == t[None, :] // CHUNK) & (t[None, :] <= t[:, None]),
        (t[:, None] // CHUNK == t[None, :] // CHUNK)], axis=0)

    log_gamma = np.log(1.0 - 2.0 ** (-5.0 - np.arange(HEADS, dtype=np.float64)))
    pos = np.arange(CHUNK, dtype=np.float64)
    dist = np.abs(pos[:, None] - pos[None, :])
    dmask = np.exp(log_gamma[:, None, None] * dist)
    dmask = dmask.transpose(1, 0, 2).reshape(CHUNK, QK)
    lg_col = log_gamma[kcol_r]
    qdec = np.exp(lg_col[None, :] * (pos + 1.0)[:, None])
    kdec = np.exp(lg_col[None, :] * (CHUNK - 1 - pos)[:, None])
    rdecay = np.tile(np.exp(lg_col * CHUNK)[:, None], (1, DV))
    reps = tile // CHUNK
    as_bf = lambda m: jnp.asarray(m.astype(np.float32), dtype=BF16)
    as_f32 = lambda m: jnp.asarray(m.astype(np.float32))
    return dict(ll=as_bf(ll), mkg=as_bf(mkg.T), mkr=as_bf(mkr.T), mv=as_bf(mv), mwg=as_bf(mwg.T), mwr=as_bf(mwr.T),
                causal=as_f32(causal), dmask=as_f32(dmask), qdec=as_f32(np.tile(qdec, (reps, 1))),
                kdec=as_f32(np.tile(kdec, (reps, 1))), rdecay=as_f32(rdecay))


def _rope_tables(seq):
    half = DK // 2
    inv = ROPE_BASE ** (-np.arange(half, dtype=np.float64) / half)
    ang = np.arange(seq, dtype=np.float64)[:, None] * inv[None, :]
    table = lambda t: jnp.asarray(np.tile(t, (1, HEADS)).astype(np.float32))
    return table(np.cos(ang)), table(np.sin(ang))


def _rope_halves_first(w):
    rows = w.shape[0]
    return w.reshape(rows, HEADS, 2, DK // 2).transpose(0, 2, 1, 3).reshape(rows, QK)


def _call_a_with_casts_kernel(*refs, n_in, n_cast, **kw):
    ins, cast_in = refs[:n_in], refs[n_in:n_in + n_cast]
    out_ref, cast_out = refs[n_in + n_cast], refs[n_in + n_cast + 1:n_in + 2 * n_cast + 1]
    scratch = refs[n_in + 2 * n_cast + 1:]
    def side_job():
        for src, dst in zip(cast_in, cast_out):
            dst[...] = src[...].astype(BF16)

    step, last = pl.program_id(0), pl.num_programs(0) - 1
    body = functools.partial(_call_a_kernel, *ins, out_ref, *scratch, side_job=side_job, **kw)
    pl.when(step < last)(functools.partial(body, flush=False))
    pl.when(step == last)(functools.partial(body, flush=True))


def _cast_block_rows(rows, max_steps):
    for block in range(BF16_SUBLANES, rows + 1, BF16_SUBLANES):
        if rows % block == 0 and rows // block <= max_steps:
            return block
    raise ValueError(f"no row block for {rows} rows in {max_steps} steps")


def _call_a(h, cos, sin, consts, g1, wg, wu, wd, gmix, win_parts, wup, bgate, gout, wout, cast_weights):
    b, s, d = h.shape
    tile = TILE_A
    tps = s // tile
    n_tiles = b * tps
    row = lambda a: a.reshape(1, -1)
    args = [h, cos, sin, row(g1), wg, wu, wd, row(gmix), *win_parts, wup, row(bgate), row(gout), wout,
            consts["ll"], consts["mkg"], consts["mkr"], consts["mv"], consts["mwg"], consts["mwr"],
            consts["causal"], consts["dmask"], consts["qdec"], consts["kdec"], consts["rdecay"]]
    proj_tile = lambda st: jnp.minimum(st, n_tiles - 1)
    mix_tile = lambda st: jnp.maximum(st - 1, 0)
    in_specs = [pl.BlockSpec((1, tile, d), lambda st: (proj_tile(st) // tps, proj_tile(st) % tps, 0)),
                pl.BlockSpec((tile, LANES), lambda st: (proj_tile(st) % tps, 0)),
                pl.BlockSpec((tile, LANES), lambda st: (proj_tile(st) % tps, 0))]
    in_specs += [_const_spec(a.shape) for a in args[3:]]
    bf = lambda *shape: pltpu.VMEM(shape, BF16)
    f32 = lambda *shape: pltpu.VMEM(shape, F32)
    kt = bf(tile // CHUNK * QK, 2 * CHUNK)
    scratch = [f32(QK, DV), f32(QK, DV), f32(tile, 2 * VW), f32(tile, d), f32(tile, VW), f32(tile, VW),
               bf(tile, QK), bf(tile, QK), kt, kt, kt, bf(tile, VW),
               f32(tile // CHUNK * QK, 2 * CHUNK),
               bf(tile, QK), bf(tile, QK), kt, kt, bf(tile, VW)]
    out_specs = [pl.BlockSpec((1, tile, d), lambda st: (mix_tile(st) // tps, mix_tile(st) % tps, 0))]
    out_shape = [jax.ShapeDtypeStruct((b, s, d), F32)]
    n_in = len(args)
    for w in cast_weights:
        rows, cols = w.shape
        block = _cast_block_rows(rows, n_tiles)
        spec = pl.BlockSpec((block, cols), functools.partial(lambda st, last: (jnp.minimum(st, last), 0),
                                                             last=rows // block - 1))
        in_specs.append(spec)
        out_specs.append(spec)
        out_shape.append(jax.ShapeDtypeStruct(w.shape, BF16))
    outs = pl.pallas_call(
        functools.partial(_call_a_with_casts_kernel, n_in=n_in, n_cast=len(cast_weights),
                          tile=tile, tiles_per_seq=tps),
        grid=(n_tiles + 1,),
        in_specs=in_specs,
        out_specs=out_specs,
        out_shape=out_shape,
        scratch_shapes=scratch,
        compiler_params=pltpu.CompilerParams(dimension_semantics=("arbitrary",),
                                             vmem_limit_bytes=VMEM_LIMIT),
        name="ffn_a_mixers",
    )(*args, *cast_weights)
    return outs[0], outs[1:]


def _call_b(h, p, g2, wg, wu, wd, gple, wpg, wpp, gfin, final):
    b, s, d = h.shape
    tile = TILE_B
    row = lambda a: a.reshape(1, -1)
    args = [h, p, row(g2), wg, wu, wd, row(gple), wpg, wpp, row(gfin)]
    in_specs = [pl.BlockSpec((1, tile, d), lambda bi, si: (bi, si, 0)),
                pl.BlockSpec((1, tile, p.shape[-1]), lambda bi, si: (bi, si, 0))]
    in_specs += [_const_spec(a.shape) for a in args[2:]]
    return pl.pallas_call(
        functools.partial(_call_b_kernel, final=final),
        grid=(b, s // tile),
        in_specs=in_specs,
        out_specs=pl.BlockSpec((1, tile, d), lambda bi, si: (bi, si, 0)),
        out_shape=jax.ShapeDtypeStruct((b, s, d), F32),
        compiler_params=pltpu.CompilerParams(dimension_semantics=("arbitrary", "arbitrary"),
                                             vmem_limit_bytes=VMEM_LIMIT),
        name="ffn_b_embed",
    )(*args)


def kernel(x, p, g_ffn1, w_ffn1_gate, w_ffn1_up, w_ffn1_down, g_mix, w_in, w_gla_gate_up, b_gla_gate, g_gla_out, g_ret_out, w_out, g_ffn2, w_ffn2_gate, w_ffn2_up, w_ffn2_down, g_ple, w_ple_gate, w_ple_proj, g_final):
    depth = p.shape[0]
    seq = x.shape[1]
    assert seq % TILE_A == 0 and seq % TILE_B == 0 and TILE_A % CHUNK == 0
    assert w_ffn1_gate.shape[2] % MXU_COLS == 0
    consts = _mixer_constants(TILE_A)
    cos, sin = _rope_tables(seq)
    sizes = [QK, QK, VW, VW, GATE_RANK, QK, QK, VW, VW]
    offs = np.concatenate([[0], np.cumsum(sizes)])
    bf = lambda a: a.astype(BF16)
    h = x
    for i in range(depth):
        wi = w_in[i]
        cols = lambda j0, j1: bf(wi[:, offs[j0]:offs[j1]])
        win_parts = [jnp.pad(cols(4, 5), ((0, 0), (0, GATE_RANK_PAD - GATE_RANK))),
                     cols(0, 2),
                     _rope_halves_first(cols(5, 6)), _rope_halves_first(cols(6, 7)),
                     cols(2, 4), cols(7, 9)]
        wup = bf(jnp.pad(w_gla_gate_up[i], ((0, GATE_RANK_PAD - GATE_RANK), (0, 0))))
        gout = jnp.concatenate([g_gla_out[i], g_ret_out[i]])
        next_weights = [w_ffn2_gate[i], w_ffn2_up[i], w_ffn2_down[i], w_ple_gate[i], w_ple_proj[i]]
        h, (wg2, wu2, wd2, wpg, wpp) = _call_a(
            h, cos, sin, consts, g_ffn1[i], bf(w_ffn1_gate[i]), bf(w_ffn1_up[i]), bf(w_ffn1_down[i]),
            g_mix[i], win_parts, wup, b_gla_gate[i], gout, bf(w_out[i]), next_weights)
        h = _call_b(h, p[i], g_ffn2[i], wg2, wu2, wd2, g_ple[i], wpg, wpp, g_final, final=(i == depth - 1))
    return h
```

```python
import functools

import jax
import jax.numpy as jnp
import numpy as np
from jax import lax
from jax.experimental import pallas as pl
from jax.experimental.pallas import tpu as pltpu

CHUNK = 64
HEADS = 4
DK = 64
DV = 128
QK = HEADS * DK
VW = HEADS * DV
GATE_RANK = 16
GATE_RANK_PAD = 128
GATE_NORM = 16.0
ROPE_BASE = 10000.0
EPS = 1e-6
LANES = 128
BF16_SUBLANES = 16
MXU_COLS = 256
FFN_PIECE_COLS = 2 * MXU_COLS

TILE_A = 512
CUM_ROWS = 256
TILE_B = 512
TAIL_GROUPS_B = 4
VMEM_LIMIT = 60 * 1024 * 1024

F32 = jnp.float32
BF16 = jnp.bfloat16


def _dot(a, b):
    return jnp.dot(a, b, preferred_element_type=F32)


def _rms(x, g):
    return x * lax.rsqrt(jnp.mean(x * x, axis=-1, keepdims=True) + EPS) * g


def _sigmoid(x):
    return 1.0 / (1.0 + jnp.exp(-x))


def _silu(x):
    return x * _sigmoid(x)


def _log_sigmoid(x):
    return jnp.minimum(x, 0.0) - jnp.log1p(jnp.exp(-jnp.abs(x)))


def _swiglu(xn, wg_ref, wu_ref, wd_ref):
    g = _dot(xn, wg_ref[...])
    u = _dot(xn, wu_ref[...])
    return _dot((_silu(g) * u).astype(BF16), wd_ref[...])


def _ffn_pieces(ff):
    return [slice(c, min(c + FFN_PIECE_COLS, ff)) for c in range(0, ff, FFN_PIECE_COLS)]


def _swiglu_up(xn, wg_ref, wu_ref, cols):
    return _dot(xn, wg_ref[:, cols]), _dot(xn, wu_ref[:, cols])


def _swiglu_down(gu, wd_ref, cols):
    g, u = gu
    return _dot((_silu(g) * u).astype(BF16), wd_ref[cols, :])


def _tile4(a):
    return jnp.concatenate([a, a, a, a], axis=0)


def _lanes2(a):
    return jnp.concatenate([a, a], axis=1)


def _lanes4(a):
    return jnp.concatenate([a, a, a, a], axis=1)


def _store_chunks_transposed(dst_ref, k):
    for c in range(k.shape[0] // CHUNK):
        blk = k[c * CHUNK:(c + 1) * CHUNK, :]
        dst_ref[c * QK:(c + 1) * QK, :] = jnp.concatenate([blk, blk], axis=0).T.astype(dst_ref.dtype)


def _call_a_kernel(x_ref, cos_ref, sin_ref, g1_ref, wg_ref, wu_ref, wd_ref, gmix_ref,
                   wlr_ref, wgqk_ref, wrq_ref, wrk_ref, wgvr_ref, wrvg_ref,
                   wup_ref, bgate_ref, gout_ref, wout_ref, ll_ref,
                   mkg_ref, mkr_ref, mv_ref, mwg_ref, mwr_ref, causal_ref, dmask_ref,
                   qdec_ref, kdec_ref, rdecay_ref,
                   out_ref,
                   st_g, st_r, o_s, h1_s, gr_s, rg_s, qf_s, qb_s, kf_s, kb_s, ks_s, gv_s, dec_s,
                   qr_s, qd_s, kr_s, kd_s, rv_s, *, tile, tiles_per_seq, flush):
    step = pl.program_id(0)
    carried = (o_s, h1_s, gr_s, rg_s, qf_s, qb_s, kf_s, kb_s, ks_s, gv_s, dec_s, qr_s, qd_s, kr_s, kd_s, rv_s)

    @pl.when(step == 0)
    def _():
        for ref in (st_g, st_r) + carried:
            ref[...] = jnp.zeros_like(ref)

    keep = lax.rem(jnp.maximum(step - 1, 0), tiles_per_seq) != 0
    st_g[...] = jnp.where(keep, st_g[...], 0.0)
    st_r[...] = jnp.where(keep, st_r[...], 0.0)

    half = DK // 2

    def gla_chunk(c):
        rows = pl.ds(c * CHUNK, CHUNK)
        qf = qf_s[rows, :]
        krows = pl.ds(c * QK, QK)
        s_f = _dot(qf, _lanes2(kf_s[krows, :]) * mkg_ref[...])
        s_b = _dot(qb_s[rows, :], _lanes2(kb_s[krows, :]) * mkg_ref[...])
        p = jnp.where(causal_ref[...] > 0.5, s_f, s_b).astype(BF16)
        v = gv_s[rows, :]
        st = st_g[...]
        w = _lanes4(st.astype(BF16)) * mwg_ref[...]
        o_s[rows, 0:VW] = _dot(p, _tile4(v) * mv_ref[...]) + _dot(qf, w)
        kt = ks_s[krows, :][:, :CHUNK]
        upd = [_dot(kt[h * DK:(h + 1) * DK, :], v[:, h * DV:(h + 1) * DV]) for h in range(HEADS)]
        st_g[...] = st * dec_s[krows, :] + jnp.concatenate(upd, axis=0)

    def ret_chunk(c):
        rows = pl.ds(c * CHUNK, CHUNK)
        s = _dot(qr_s[rows, :], _lanes2(kr_s[pl.ds(c * QK, QK), :]) * mkr_ref[...]) * dmask_ref[...]
        v = rv_s[rows, :]
        st = st_r[...]
        w = _lanes4(st.astype(BF16)) * mwr_ref[...]
        o_s[rows, VW:2 * VW] = _dot(s.astype(BF16), _tile4(v) * mv_ref[...]) + _dot(qd_s[rows, :], w)
        kt = kd_s[pl.ds(c * QK, QK), :][:, :CHUNK]
        upd = []
        for h in range(HEADS):
            k_h = jnp.concatenate([kt[h * half:(h + 1) * half, :],
                                   kt[LANES + h * half:LANES + (h + 1) * half, :]], axis=0)
            upd.append(_dot(k_h, v[:, h * DV:(h + 1) * DV]))
        upd = [u[:half] for u in upd] + [u[half:] for u in upd]
        st_r[...] = st * rdecay_ref[...] + jnp.concatenate(upd, axis=0)

    def mix_operand():
        gout = gout_ref[...]
        gates = (gr_s, rg_s)
        parts = []
        for h in range(2 * HEADS):
            oh = o_s[:, h * DV:(h + 1) * DV]
            gate = gates[h // HEADS][:, (h % HEADS) * DV:(h % HEADS + 1) * DV]
            parts.append((_rms(oh, gout[:, h * DV:(h + 1) * DV]) * _silu(gate)).astype(BF16))
        return jnp.concatenate(parts, axis=1)

    mixer_units = []
    for c in range(tile // CHUNK):
        mixer_units += [functools.partial(gla_chunk, c), functools.partial(ret_chunk, c)]

    if flush:
        for unit in mixer_units:
            unit()
        out_ref[0] = h1_s[...] + _dot(mix_operand(), wout_ref[...])
        return

    mixer_units[0]()
    x = x_ref[0]
    xn = _rms(x, g1_ref[...]).astype(BF16)
    pieces = _ffn_pieces(wd_ref.shape[0])
    n_pieces = len(pieces)
    spread = max(n_pieces - 2, 1)
    n_rest = len(mixer_units) - 1
    f = None
    mix = None
    gu = _swiglu_up(xn, wg_ref, wu_ref, pieces[0])
    for j in range(n_pieces):
        gu_next = _swiglu_up(xn, wg_ref, wu_ref, pieces[j + 1]) if j + 1 < n_pieces else None
        fj = _swiglu_down(gu, wd_ref, pieces[j])
        f = fj if f is None else f + fj
        gu = gu_next
        if j < spread:
            for u in range(1 + (j * n_rest) // spread, 1 + ((j + 1) * n_rest) // spread):
                mixer_units[u]()
        if j == spread - 1:
            mix = mix_operand()
    out_ref[0] = h1_s[...] + _dot(mix, wout_ref[...])
    h1 = x + 0.5 * f
    n = _rms(h1, gmix_ref[...]).astype(BF16)

    glr = _dot(n, wlr_ref[...])
    gqk = _dot(n, wgqk_ref[...])
    rq = _dot(n, wrq_ref[...])
    rk = _dot(n, wrk_ref[...])
    gq = gqk[:, :QK]
    gk = gqk[:, QK:]

    logit = _dot(glr.astype(BF16), wup_ref[...]) + bgate_ref[...]
    log_a = _log_sigmoid(logit) * (1.0 / GATE_NORM)
    la_hi = log_a.astype(BF16)
    la_lo = (log_a - la_hi.astype(F32)).astype(BF16)
    ll = ll_ref[...]
    cums, cls = [], []
    for r in range(tile // CUM_ROWS):
        blk = slice(r * CUM_ROWS, (r + 1) * CUM_ROWS)
        cc = _dot(ll, la_hi[blk]) + _dot(ll, la_lo[blk])
        cums.append(cc[:CUM_ROWS])
        cls.append(cc[CUM_ROWS:])
    cum = jnp.concatenate(cums, axis=0)
    cl = jnp.concatenate(cls, axis=0)
    e_pos = jnp.exp(cum)
    e_neg = jnp.exp(-cum)
    qs = gq * (DK ** -0.5)

    cos = cos_ref[...]
    sin = sin_ref[...]

    def rope(t):
        t1 = t[:, :LANES]
        t2 = t[:, LANES:]
        return jnp.concatenate([t1 * cos - t2 * sin, t2 * cos + t1 * sin], axis=1)

    qr = rope(rq) * (DK ** -0.5)
    kr = rope(rk)

    h1_s[...] = h1
    qf_s[...] = (qs * e_pos).astype(BF16)
    qb_s[...] = (qs * e_neg).astype(BF16)
    _store_chunks_transposed(kf_s, gk * e_neg)
    _store_chunks_transposed(kb_s, gk * e_pos)
    _store_chunks_transposed(ks_s, gk * jnp.exp(cl - cum))
    _store_chunks_transposed(dec_s, jnp.exp(cl))
    qr_s[...] = qr.astype(BF16)
    qd_s[...] = (qr * qdec_ref[...]).astype(BF16)
    _store_chunks_transposed(kr_s, kr)
    _store_chunks_transposed(kd_s, kr * kdec_ref[...])

    gvr = _dot(n, wgvr_ref[...])
    gv_s[...] = gvr[:, :VW].astype(BF16)
    gr_s[...] = gvr[:, VW:]
    rvg = _dot(n, wrvg_ref[...])
    rv_s[...] = rvg[:, :VW].astype(BF16)
    rg_s[...] = rvg[:, VW:]


def _call_b_kernel(h_ref, p_ref, g2_ref, wg_ref, wu_ref, wd_ref, gple_ref, wpg_ref, wpp_ref, gfin_ref,
                   out_ref, *, final):
    h = h_ref[0]
    f = _swiglu(_rms(h, g2_ref[...]).astype(BF16), wg_ref, wu_ref, wd_ref)
    pe = _dot(p_ref[0].astype(BF16), wpp_ref[...])
    h3 = h + 0.5 * f
    n3 = _rms(h3, gple_ref[...]).astype(BF16)
    group_rows = h.shape[0] // TAIL_GROUPS_B
    groups = [slice(k * group_rows, (k + 1) * group_rows) for k in range(TAIL_GROUPS_B)]
    gates = [_dot(n3[rows], wpg_ref[...]) for rows in groups]
    for rows, gate in zip(groups, gates):
        h4 = h3[rows] + _sigmoid(gate) * pe[rows]
        out_ref[0, rows, :] = _rms(h4, gfin_ref[...]) if final else h4


def _const_spec(shape):
    return pl.BlockSpec(shape, lambda *_: (0,) * len(shape), pipeline_mode=pl.Buffered(1))


def _mixer_constants(tile):
    r = np.arange(QK)
    kcol_g = r // DK
    kcol_r = (r % LANES) // (DK // 2)
    row_h = r // CHUNK
    mkg = (row_h[:, None] == kcol_g[None, :])
    mkr = (row_h[:, None] == kcol_r[None, :])
    mv = (row_h[:, None] == (np.arange(VW) // DV)[None, :])
    vrow_h = np.arange(VW) // DV
    mwg = (vrow_h[:, None] == kcol_g[None, :])
    mwr = (vrow_h[:, None] == kcol_r[None, :])
    i = np.arange(CHUNK)
    j = r % CHUNK
    causal = (i[:, None] >= j[None, :]).astype(np.float32)
    t = np.arange(CUM_ROWS)
    ll = np.concatenate([
        (t[:, None] // CHUNK == t[None, :] // CHUNK) & (t[None, :] <= t[:, None]),
        (t[:, None] // CHUNK == t[None, :] // CHUNK)], axis=0)

    log_gamma = np.log(1.0 - 2.0 ** (-5.0 - np.arange(HEADS, dtype=np.float64)))
    pos = np.arange(CHUNK, dtype=np.float64)
    dist = np.abs(pos[:, None] - pos[None, :])
    dmask = np.exp(log_gamma[:, None, None] * dist)
    dmask = dmask.transpose(1, 0, 2).reshape(CHUNK, QK)
    lg_col = log_gamma[kcol_r]
    qdec = np.exp(lg_col[None, :] * (pos + 1.0)[:, None])
    kdec = np.exp(lg_col[None, :] * (CHUNK - 1 - pos)[:, None])
    rdecay = np.tile(np.exp(lg_col * CHUNK)[:, None], (1, DV))
    reps = tile // CHUNK
    as_bf = lambda m: jnp.asarray(m.astype(np.float32), dtype=BF16)
    as_f32 = lambda m: jnp.asarray(m.astype(np.float32))
    return dict(ll=as_bf(ll), mkg=as_bf(mkg.T), mkr=as_bf(mkr.T), mv=as_bf(mv), mwg=as_bf(mwg.T), mwr=as_bf(mwr.T),
                causal=as_f32(causal), dmask=as_f32(dmask), qdec=as_f32(np.tile(qdec, (reps, 1))),
                kdec=as_f32(np.tile(kdec, (reps, 1))), rdecay=as_f32(rdecay))


def _rope_tables(seq):
    half = DK // 2
    inv = ROPE_BASE ** (-np.arange(half, dtype=np.float64) / half)
    ang = np.arange(seq, dtype=np.float64)[:, None] * inv[None, :]
    table = lambda t: jnp.asarray(np.tile(t, (1, HEADS)).astype(np.float32))
    return table(np.cos(ang)), table(np.sin(ang))


def _rope_halves_first(w):
    rows = w.shape[0]
    return w.reshape(rows, HEADS, 2, DK // 2).transpose(0, 2, 1, 3).reshape(rows, QK)


def _call_a_with_casts_kernel(*refs, n_in, n_cast, **kw):
    ins, cast_in = refs[:n_in], refs[n_in:n_in + n_cast]
    out_ref, cast_out = refs[n_in + n_cast], refs[n_in + n_cast + 1:n_in + 2 * n_cast + 1]
    scratch = refs[n_in + 2 * n_cast + 1:]
    for src, dst in zip(cast_in, cast_out):
        dst[...] = src[...].astype(BF16)
    step, last = pl.program_id(0), pl.num_programs(0) - 1
    pl.when(step < last)(functools.partial(_call_a_kernel, *ins, out_ref, *scratch, flush=False, **kw))
    pl.when(step == last)(functools.partial(_call_a_kernel, *ins, out_ref, *scratch, flush=True, **kw))


def _cast_block_rows(rows, max_steps):
    for block in range(BF16_SUBLANES, rows + 1, BF16_SUBLANES):
        if rows % block == 0 and rows // block <= max_steps:
            return block
    raise ValueError(f"no row block for {rows} rows in {max_steps} steps")


def _call_a(h, cos, sin, consts, g1, wg, wu, wd, gmix, win_parts, wup, bgate, gout, wout, cast_weights):
    b, s, d = h.shape
    tile = TILE_A
    tps = s // tile
    n_tiles = b * tps
    row = lambda a: a.reshape(1, -1)
    args = [h, cos, sin, row(g1), wg, wu, wd, row(gmix), *win_parts, wup, row(bgate), row(gout), wout,
            consts["ll"], consts["mkg"], consts["mkr"], consts["mv"], consts["mwg"], consts["mwr"],
            consts["causal"], consts["dmask"], consts["qdec"], consts["kdec"], consts["rdecay"]]
    proj_tile = lambda st: jnp.minimum(st, n_tiles - 1)
    mix_tile = lambda st: jnp.maximum(st - 1, 0)
    in_specs = [pl.BlockSpec((1, tile, d), lambda st: (proj_tile(st) // tps, proj_tile(st) % tps, 0)),
                pl.BlockSpec((tile, LANES), lambda st: (proj_tile(st) % tps, 0)),
                pl.BlockSpec((tile, LANES), lambda st: (proj_tile(st) % tps, 0))]
    in_specs += [_const_spec(a.shape) for a in args[3:]]
    bf = lambda *shape: pltpu.VMEM(shape, BF16)
    f32 = lambda *shape: pltpu.VMEM(shape, F32)
    kt = bf(tile // CHUNK * QK, 2 * CHUNK)
    scratch = [f32(QK, DV), f32(QK, DV), f32(tile, 2 * VW), f32(tile, d), f32(tile, VW), f32(tile, VW),
               bf(tile, QK), bf(tile, QK), kt, kt, kt, bf(tile, VW),
               f32(tile // CHUNK * QK, 2 * CHUNK),
               bf(tile, QK), bf(tile, QK), kt, kt, bf(tile, VW)]
    out_specs = [pl.BlockSpec((1, tile, d), lambda st: (mix_tile(st) // tps, mix_tile(st) % tps, 0))]
    out_shape = [jax.ShapeDtypeStruct((b, s, d), F32)]
    n_in = len(args)
    for w in cast_weights:
        rows, cols = w.shape
        block = _cast_block_rows(rows, n_tiles)
        spec = pl.BlockSpec((block, cols), functools.partial(lambda st, last: (jnp.minimum(st, last), 0),
                                                             last=rows // block - 1))
        in_specs.append(spec)
        out_specs.append(spec)
        out_shape.append(jax.ShapeDtypeStruct(w.shape, BF16))
    outs = pl.pallas_call(
        functools.partial(_call_a_with_casts_kernel, n_in=n_in, n_cast=len(cast_weights),
                          tile=tile, tiles_per_seq=tps),
        grid=(n_tiles + 1,),
        in_specs=in_specs,
        out_specs=out_specs,
        out_shape=out_shape,
        scratch_shapes=scratch,
        compiler_params=pltpu.CompilerParams(dimension_semantics=("arbitrary",),
                                             vmem_limit_bytes=VMEM_LIMIT),
        name="ffn_a_mixers",
    )(*args, *cast_weights)
    return outs[0], outs[1:]


def _call_b(h, p, g2, wg, wu, wd, gple, wpg, wpp, gfin, final):
    b, s, d = h.shape
    tile = TILE_B
    row = lambda a: a.reshape(1, -1)
    args = [h, p, row(g2), wg, wu, wd, row(gple), wpg, wpp, row(gfin)]
    in_specs = [pl.BlockSpec((1, tile, d), lambda bi, si: (bi, si, 0)),
                pl.BlockSpec((1, tile, p.shape[-1]), lambda bi, si: (bi, si, 0))]
    in_specs += [_const_spec(a.shape) for a in args[2:]]
    return pl.pallas_call(
        functools.partial(_call_b_kernel, final=final),
        grid=(b, s // tile),
        in_specs=in_specs,
        out_specs=pl.BlockSpec((1, tile, d), lambda bi, si: (bi, si, 0)),
        out_shape=jax.ShapeDtypeStruct((b, s, d), F32),
        compiler_params=pltpu.CompilerParams(dimension_semantics=("arbitrary", "arbitrary"),
                                             vmem_limit_bytes=VMEM_LIMIT),
        name="ffn_b_embed",
    )(*args)


def kernel(x, p, g_ffn1, w_ffn1_gate, w_ffn1_up, w_ffn1_down, g_mix, w_in, w_gla_gate_up, b_gla_gate, g_gla_out, g_ret_out, w_out, g_ffn2, w_ffn2_gate, w_ffn2_up, w_ffn2_down, g_ple, w_ple_gate, w_ple_proj, g_final):
    depth = p.shape[0]
    seq = x.shape[1]
    assert seq % TILE_A == 0 and seq % TILE_B == 0 and TILE_A % CHUNK == 0
    assert w_ffn1_gate.shape[2] % MXU_COLS == 0
    consts = _mixer_constants(TILE_A)
    cos, sin = _rope_tables(seq)
    sizes = [QK, QK, VW, VW, GATE_RANK, QK, QK, VW, VW]
    offs = np.concatenate([[0], np.cumsum(sizes)])
    bf = lambda a: a.astype(BF16)
    h = x
    for i in range(depth):
        wi = w_in[i]
        cols = lambda j0, j1: bf(wi[:, offs[j0]:offs[j1]])
        win_parts = [jnp.pad(cols(4, 5), ((0, 0), (0, GATE_RANK_PAD - GATE_RANK))),
                     cols(0, 2),
                     _rope_halves_first(cols(5, 6)), _rope_halves_first(cols(6, 7)),
                     cols(2, 4), cols(7, 9)]
        wup = bf(jnp.pad(w_gla_gate_up[i], ((0, GATE_RANK_PAD - GATE_RANK), (0, 0))))
        gout = jnp.concatenate([g_gla_out[i], g_ret_out[i]])
        next_weights = [w_ffn2_gate[i], w_ffn2_up[i], w_ffn2_down[i], w_ple_gate[i], w_ple_proj[i]]
        h, (wg2, wu2, wd2, wpg, wpp) = _call_a(
            h, cos, sin, consts, g_ffn1[i], bf(w_ffn1_gate[i]), bf(w_ffn1_up[i]), bf(w_ffn1_down[i]),
            g_mix[i], win_parts, wup, b_gla_gate[i], gout, bf(w_out[i]), next_weights)
        h = _call_b(h, p[i], g_ffn2[i], wg2, wu2, wd2, g_ple[i], wpg, wpp, g_final, final=(i == depth - 1))
    return h
```

```python
import functools

import jax
import jax.numpy as jnp
import numpy as np
from jax import lax
from jax.experimental import pallas as pl
from jax.experimental.pallas import tpu as pltpu

CHUNK = 64
HEADS = 4
DK = 64
DV = 128
QK = HEADS * DK
VW = HEADS * DV
GATE_RANK = 16
GATE_RANK_PAD = 128
GATE_NORM = 16.0
ROPE_BASE = 10000.0
EPS = 1e-6
LANES = 128
BF16_SUBLANES = 16
MXU_COLS = 256
FFN_PIECE_COLS = 3 * MXU_COLS

TILE_A = 512
CUM_ROWS = 256
TILE_B = 512
TAIL_GROUPS_B = 2
VMEM_LIMIT = 60 * 1024 * 1024

F32 = jnp.float32
BF16 = jnp.bfloat16


def _dot(a, b):
    return jnp.dot(a, b, preferred_element_type=F32)


def _rms(x, g):
    return x * lax.rsqrt(jnp.mean(x * x, axis=-1, keepdims=True) + EPS) * g


def _sigmoid(x):
    return 1.0 / (1.0 + jnp.exp(-x))


def _silu(x):
    return x * _sigmoid(x)


def _log_sigmoid(x):
    return jnp.minimum(x, 0.0) - jnp.log1p(jnp.exp(-jnp.abs(x)))


def _swiglu(xn, wg_ref, wu_ref, wd_ref):
    g = _dot(xn, wg_ref[...])
    u = _dot(xn, wu_ref[...])
    return _dot((_silu(g) * u).astype(BF16), wd_ref[...])


def _ffn_pieces(ff):
    return [slice(c, min(c + FFN_PIECE_COLS, ff)) for c in range(0, ff, FFN_PIECE_COLS)]


def _swiglu_up(xn, wg_ref, wu_ref, cols):
    return _dot(xn, wg_ref[:, cols]), _dot(xn, wu_ref[:, cols])


def _swiglu_down(gu, wd_ref, cols):
    g, u = gu
    return _dot((_silu(g) * u).astype(BF16), wd_ref[cols, :])


def _tile4(a):
    return jnp.concatenate([a, a, a, a], axis=0)


def _lanes2(a):
    return jnp.concatenate([a, a], axis=1)


def _lanes4(a):
    return jnp.concatenate([a, a, a, a], axis=1)


def _store_chunks_transposed(dst_ref, k):
    for c in range(k.shape[0] // CHUNK):
        blk = k[c * CHUNK:(c + 1) * CHUNK, :]
        dst_ref[c * QK:(c + 1) * QK, :] = jnp.concatenate([blk, blk], axis=0).T.astype(dst_ref.dtype)


def _call_a_kernel(x_ref, cos_ref, sin_ref, g1_ref, wg_ref, wu_ref, wd_ref, gmix_ref,
                   wlr_ref, wgqk_ref, wrq_ref, wrk_ref, wgvr_ref, wrvg_ref,
                   wup_ref, bgate_ref, gout_ref, wout_ref, ll_ref,
                   mkg_ref, mkr_ref, mv_ref, mwg_ref, mwr_ref, causal_ref, dmask_ref,
                   qdec_ref, kdec_ref, rdecay_ref,
                   out_ref,
                   st_g, st_r, o_s, h1_s, gr_s, rg_s, qf_s, qb_s, kf_s, kb_s, ks_s, gv_s, dec_s,
                   qr_s, qd_s, kr_s, kd_s, rv_s, *, tile, tiles_per_seq, flush):
    step = pl.program_id(0)
    carried = (o_s, h1_s, gr_s, rg_s, qf_s, qb_s, kf_s, kb_s, ks_s, gv_s, dec_s, qr_s, qd_s, kr_s, kd_s, rv_s)

    @pl.when(step == 0)
    def _():
        for ref in (st_g, st_r) + carried:
            ref[...] = jnp.zeros_like(ref)

    keep = lax.rem(jnp.maximum(step - 1, 0), tiles_per_seq) != 0
    st_g[...] = jnp.where(keep, st_g[...], 0.0)
    st_r[...] = jnp.where(keep, st_r[...], 0.0)

    half = DK // 2

    def gla_chunk(c):
        rows = pl.ds(c * CHUNK, CHUNK)
        qf = qf_s[rows, :]
        krows = pl.ds(c * QK, QK)
        s_f = _dot(qf, _lanes2(kf_s[krows, :]) * mkg_ref[...])
        s_b = _dot(qb_s[rows, :], _lanes2(kb_s[krows, :]) * mkg_ref[...])
        p = jnp.where(causal_ref[...] > 0.5, s_f, s_b).astype(BF16)
        v = gv_s[rows, :]
        st = st_g[...]
        w = _lanes4(st.astype(BF16)) * mwg_ref[...]
        o_s[rows, 0:VW] = _dot(p, _tile4(v) * mv_ref[...]) + _dot(qf, w)
        kt = ks_s[krows, :][:, :CHUNK]
        upd = [_dot(kt[h * DK:(h + 1) * DK, :], v[:, h * DV:(h + 1) * DV]) for h in range(HEADS)]
        st_g[...] = st * dec_s[krows, :] + jnp.concatenate(upd, axis=0)

    def ret_chunk(c):
        rows = pl.ds(c * CHUNK, CHUNK)
        s = _dot(qr_s[rows, :], _lanes2(kr_s[pl.ds(c * QK, QK), :]) * mkr_ref[...]) * dmask_ref[...]
        v = rv_s[rows, :]
        st = st_r[...]
        w = _lanes4(st.astype(BF16)) * mwr_ref[...]
        o_s[rows, VW:2 * VW] = _dot(s.astype(BF16), _tile4(v) * mv_ref[...]) + _dot(qd_s[rows, :], w)
        kt = kd_s[pl.ds(c * QK, QK), :][:, :CHUNK]
        upd = []
        for h in range(HEADS):
            k_h = jnp.concatenate([kt[h * half:(h + 1) * half, :],
                                   kt[LANES + h * half:LANES + (h + 1) * half, :]], axis=0)
            upd.append(_dot(k_h, v[:, h * DV:(h + 1) * DV]))
        upd = [u[:half] for u in upd] + [u[half:] for u in upd]
        st_r[...] = st * rdecay_ref[...] + jnp.concatenate(upd, axis=0)

    def mix_operand():
        gout = gout_ref[...]
        gates = (gr_s, rg_s)
        parts = []
        for h in range(2 * HEADS):
            oh = o_s[:, h * DV:(h + 1) * DV]
            gate = gates[h // HEADS][:, (h % HEADS) * DV:(h % HEADS + 1) * DV]
            parts.append((_rms(oh, gout[:, h * DV:(h + 1) * DV]) * _silu(gate)).astype(BF16))
        return jnp.concatenate(parts, axis=1)

    mixer_units = []
    for c in range(tile // CHUNK):
        mixer_units += [functools.partial(gla_chunk, c), functools.partial(ret_chunk, c)]

    if flush:
        for unit in mixer_units:
            unit()
        out_ref[0] = h1_s[...] + _dot(mix_operand(), wout_ref[...])
        return

    mixer_units[0]()
    x = x_ref[0]
    xn = _rms(x, g1_ref[...]).astype(BF16)
    pieces = _ffn_pieces(wd_ref.shape[0])
    n_pieces = len(pieces)
    spread = max(n_pieces - 2, 1)
    n_rest = len(mixer_units) - 1
    f = None
    mix = None
    gu = _swiglu_up(xn, wg_ref, wu_ref, pieces[0])
    for j in range(n_pieces):
        gu_next = _swiglu_up(xn, wg_ref, wu_ref, pieces[j + 1]) if j + 1 < n_pieces else None
        fj = _swiglu_down(gu, wd_ref, pieces[j])
        f = fj if f is None else f + fj
        gu = gu_next
        if j < spread:
            for u in range(1 + (j * n_rest) // spread, 1 + ((j + 1) * n_rest) // spread):
                mixer_units[u]()
        if j == spread - 1:
            mix = mix_operand()
    out_ref[0] = h1_s[...] + _dot(mix, wout_ref[...])
    h1 = x + 0.5 * f
    n = _rms(h1, gmix_ref[...]).astype(BF16)

    glr = _dot(n, wlr_ref[...])
    gqk = _dot(n, wgqk_ref[...])
    rq = _dot(n, wrq_ref[...])
    rk = _dot(n, wrk_ref[...])
    gq = gqk[:, :QK]
    gk = gqk[:, QK:]

    logit = _dot(glr.astype(BF16), wup_ref[...]) + bgate_ref[...]
    log_a = _log_sigmoid(logit) * (1.0 / GATE_NORM)
    la_hi = log_a.astype(BF16)
    la_lo = (log_a - la_hi.astype(F32)).astype(BF16)
    ll = ll_ref[...]
    cums, cls = [], []
    for r in range(tile // CUM_ROWS):
        blk = slice(r * CUM_ROWS, (r + 1) * CUM_ROWS)
        cc = _dot(ll, la_hi[blk]) + _dot(ll, la_lo[blk])
        cums.append(cc[:CUM_ROWS])
        cls.append(cc[CUM_ROWS:])
    cum = jnp.concatenate(cums, axis=0)
    cl = jnp.concatenate(cls, axis=0)
    e_pos = jnp.exp(cum)
    e_neg = jnp.exp(-cum)
    qs = gq * (DK ** -0.5)

    cos = cos_ref[...]
    sin = sin_ref[...]

    def rope(t):
        t1 = t[:, :LANES]
        t2 = t[:, LANES:]
        return jnp.concatenate([t1 * cos - t2 * sin, t2 * cos + t1 * sin], axis=1)

    qr = rope(rq) * (DK ** -0.5)
    kr = rope(rk)

    h1_s[...] = h1
    qf_s[...] = (qs * e_pos).astype(BF16)
    qb_s[...] = (qs * e_neg).astype(BF16)
    _store_chunks_transposed(kf_s, gk * e_neg)
    _store_chunks_transposed(kb_s, gk * e_pos)
    _store_chunks_transposed(ks_s, gk * jnp.exp(cl - cum))
    _store_chunks_transposed(dec_s, jnp.exp(cl))
    qr_s[...] = qr.astype(BF16)
    qd_s[...] = (qr * qdec_ref[...]).astype(BF16)
    _store_chunks_transposed(kr_s, kr)
    _store_chunks_transposed(kd_s, kr * kdec_ref[...])

    gvr = _dot(n, wgvr_ref[...])
    gv_s[...] = gvr[:, :VW].astype(BF16)
    gr_s[...] = gvr[:, VW:]
    rvg = _dot(n, wrvg_ref[...])
    rv_s[...] = rvg[:, :VW].astype(BF16)
    rg_s[...] = rvg[:, VW:]


def _call_b_kernel(h_ref, p_ref, g2_ref, wg_ref, wu_ref, wd_ref, gple_ref, wpg_ref, wpp_ref, gfin_ref,
                   out_ref, *, final):
    h = h_ref[0]
    f = _swiglu(_rms(h, g2_ref[...]).astype(BF16), wg_ref, wu_ref, wd_ref)
    pe = _dot(p_ref[0].astype(BF16), wpp_ref[...])
    h3 = h + 0.5 * f
    n3 = _rms(h3, gple_ref[...]).astype(BF16)
    group_rows = h.shape[0] // TAIL_GROUPS_B
    groups = [slice(k * group_rows, (k + 1) * group_rows) for k in range(TAIL_GROUPS_B)]
    gates = [_dot(n3[rows], wpg_ref[...]) for rows in groups]
    for rows, gate in zip(groups, gates):
        h4 = h3[rows] + _sigmoid(gate) * pe[rows]
        out_ref[0, rows, :] = _rms(h4, gfin_ref[...]) if final else h4


def _const_spec(shape):
    return pl.BlockSpec(shape, lambda *_: (0,) * len(shape), pipeline_mode=pl.Buffered(1))


def _mixer_constants(tile):
    r = np.arange(QK)
    kcol_g = r // DK
    kcol_r = (r % LANES) // (DK // 2)
    row_h = r // CHUNK
    mkg = (row_h[:, None] == kcol_g[None, :])
    mkr = (row_h[:, None] == kcol_r[None, :])
    mv = (row_h[:, None] == (np.arange(VW) // DV)[None, :])
    vrow_h = np.arange(VW) // DV
    mwg = (vrow_h[:, None] == kcol_g[None, :])
    mwr = (vrow_h[:, None] == kcol_r[None, :])
    i = np.arange(CHUNK)
    j = r % CHUNK
    causal = (i[:, None] >= j[None, :]).astype(np.float32)
    t = np.arange(CUM_ROWS)
    ll = np.concatenate([
        (t[:, None] // CHUNK == t[None, :] // CHUNK) & (t[None, :] <= t[:, None]),
        (t[:, None] // CHUNK == t[None, :] // CHUNK)], axis=0)

    log_gamma = np.log(1.0 - 2.0 ** (-5.0 - np.arange(HEADS, dtype=np.float64)))
    pos = np.arange(CHUNK, dtype=np.float64)
    dist = np.abs(pos[:, None] - pos[None, :])
    dmask = np.exp(log_gamma[:, None, None] * dist)
    dmask = dmask.transpose(1, 0, 2).reshape(CHUNK, QK)
    lg_col = log_gamma[kcol_r]
    qdec = np.exp(lg_col[None, :] * (pos + 1.0)[:, None])
    kdec = np.exp(lg_col[None, :] * (CHUNK - 1 - pos)[:, None])
    rdecay = np.tile(np.exp(lg_col * CHUNK)[:, None], (1, DV))
    reps = tile // CHUNK
    as_bf = lambda m: jnp.asarray(m.astype(np.float32), dtype=BF16)
    as_f32 = lambda m: jnp.asarray(m.astype(np.float32))
    return dict(ll=as_bf(ll), mkg=as_bf(mkg.T), mkr=as_bf(mkr.T), mv=as_bf(mv), mwg=as_bf(mwg.T), mwr=as_bf(mwr.T),
                causal=as_f32(causal), dmask=as_f32(dmask), qdec=as_f32(np.tile(qdec, (reps, 1))),
                kdec=as_f32(np.tile(kdec, (reps, 1))), rdecay=as_f32(rdecay))


def _rope_tables(seq):
    half = DK // 2
    inv = ROPE_BASE ** (-np.arange(half, dtype=np.float64) / half)
    ang = np.arange(seq, dtype=np.float64)[:, None] * inv[None, :]
    table = lambda t: jnp.asarray(np.tile(t, (1, HEADS)).astype(np.float32))
    return table(np.cos(ang)), table(np.sin(ang))


def _rope_halves_first(w):
    rows = w.shape[0]
    return w.reshape(rows, HEADS, 2, DK // 2).transpose(0, 2, 1, 3).reshape(rows, QK)


def _call_a_with_casts_kernel(*refs, n_in, n_cast, **kw):
    ins, cast_in = refs[:n_in], refs[n_in:n_in + n_cast]
    out_ref, cast_out = refs[n_in + n_cast], refs[n_in + n_cast + 1:n_in + 2 * n_cast + 1]
    scratch = refs[n_in + 2 * n_cast + 1:]
    for src, dst in zip(cast_in, cast_out):
        dst[...] = src[...].astype(BF16)
    step, last = pl.program_id(0), pl.num_programs(0) - 1
    pl.when(step < last)(functools.partial(_call_a_kernel, *ins, out_ref, *scratch, flush=False, **kw))
    pl.when(step == last)(functools.partial(_call_a_kernel, *ins, out_ref, *scratch, flush=True, **kw))


def _cast_block_rows(rows, max_steps):
    for block in range(BF16_SUBLANES, rows + 1, BF16_SUBLANES):
        if rows % block == 0 and rows // block <= max_steps:
            return block
    raise ValueError(f"no row block for {rows} rows in {max_steps} steps")


def _call_a(h, cos, sin, consts, g1, wg, wu, wd, gmix, win_parts, wup, bgate, gout, wout, cast_weights):
    b, s, d = h.shape
    tile = TILE_A
    tps = s // tile
    n_tiles = b * tps
    row = lambda a: a.reshape(1, -1)
    args = [h, cos, sin, row(g1), wg, wu, wd, row(gmix), *win_parts, wup, row(bgate), row(gout), wout,
            consts["ll"], consts["mkg"], consts["mkr"], consts["mv"], consts["mwg"], consts["mwr"],
            consts["causal"], consts["dmask"], consts["qdec"], consts["kdec"], consts["rdecay"]]
    proj_tile = lambda st: jnp.minimum(st, n_tiles - 1)
    mix_tile = lambda st: jnp.maximum(st - 1, 0)
    in_specs = [pl.BlockSpec((1, tile, d), lambda st: (proj_tile(st) // tps, proj_tile(st) % tps, 0)),
                pl.BlockSpec((tile, LANES), lambda st: (proj_tile(st) % tps, 0)),
                pl.BlockSpec((tile, LANES), lambda st: (proj_tile(st) % tps, 0))]
    in_specs += [_const_spec(a.shape) for a in args[3:]]
    bf = lambda *shape: pltpu.VMEM(shape, BF16)
    f32 = lambda *shape: pltpu.VMEM(shape, F32)
    kt = bf(tile // CHUNK * QK, 2 * CHUNK)
    scratch = [f32(QK, DV), f32(QK, DV), f32(tile, 2 * VW), f32(tile, d), f32(tile, VW), f32(tile, VW),
               bf(tile, QK), bf(tile, QK), kt, kt, kt, bf(tile, VW),
               f32(tile // CHUNK * QK, 2 * CHUNK),
               bf(tile, QK), bf(tile, QK), kt, kt, bf(tile, VW)]
    out_specs = [pl.BlockSpec((1, tile, d), lambda st: (mix_tile(st) // tps, mix_tile(st) % tps, 0))]
    out_shape = [jax.ShapeDtypeStruct((b, s, d), F32)]
    n_in = len(args)
    for w in cast_weights:
        rows, cols = w.shape
        block = _cast_block_rows(rows, n_tiles)
        spec = pl.BlockSpec((block, cols), functools.partial(lambda st, last: (jnp.minimum(st, last), 0),
                                                             last=rows // block - 1))
        in_specs.append(spec)
        out_specs.append(spec)
        out_shape.append(jax.ShapeDtypeStruct(w.shape, BF16))
    outs = pl.pallas_call(
        functools.partial(_call_a_with_casts_kernel, n_in=n_in, n_cast=len(cast_weights),
                          tile=tile, tiles_per_seq=tps),
        grid=(n_tiles + 1,),
        in_specs=in_specs,
        out_specs=out_specs,
        out_shape=out_shape,
        scratch_shapes=scratch,
        compiler_params=pltpu.CompilerParams(dimension_semantics=("arbitrary",),
                                             vmem_limit_bytes=VMEM_LIMIT),
        name="ffn_a_mixers",
    )(*args, *cast_weights)
    return outs[0], outs[1:]


def _call_b(h, p, g2, wg, wu, wd, gple, wpg, wpp, gfin, final):
    b, s, d = h.shape
    tile = TILE_B
    row = lambda a: a.reshape(1, -1)
    args = [h, p, row(g2), wg, wu, wd, row(gple), wpg, wpp, row(gfin)]
    in_specs = [pl.BlockSpec((1, tile, d), lambda bi, si: (bi, si, 0)),
                pl.BlockSpec((1, tile, p.shape[-1]), lambda bi, si: (bi, si, 0))]
    in_specs += [_const_spec(a.shape) for a in args[2:]]
    return pl.pallas_call(
        functools.partial(_call_b_kernel, final=final),
        grid=(b, s // tile),
        in_specs=in_specs,
        out_specs=pl.BlockSpec((1, tile, d), lambda bi, si: (bi, si, 0)),
        out_shape=jax.ShapeDtypeStruct((b, s, d), F32),
        compiler_params=pltpu.CompilerParams(dimension_semantics=("arbitrary", "arbitrary"),
                                             vmem_limit_bytes=VMEM_LIMIT),
        name="ffn_b_embed",
    )(*args)


def kernel(x, p, g_ffn1, w_ffn1_gate, w_ffn1_up, w_ffn1_down, g_mix, w_in, w_gla_gate_up, b_gla_gate, g_gla_out, g_ret_out, w_out, g_ffn2, w_ffn2_gate, w_ffn2_up, w_ffn2_down, g_ple, w_ple_gate, w_ple_proj, g_final):
    depth = p.shape[0]
    seq = x.shape[1]
    assert seq % TILE_A == 0 and seq % TILE_B == 0 and TILE_A % CHUNK == 0
    assert w_ffn1_gate.shape[2] % MXU_COLS == 0
    consts = _mixer_constants(TILE_A)
    cos, sin = _rope_tables(seq)
    sizes = [QK, QK, VW, VW, GATE_RANK, QK, QK, VW, VW]
    offs = np.concatenate([[0], np.cumsum(sizes)])
    bf = lambda a: a.astype(BF16)
    h = x
    for i in range(depth):
        wi = w_in[i]
        cols = lambda j0, j1: bf(wi[:, offs[j0]:offs[j1]])
        win_parts = [jnp.pad(cols(4, 5), ((0, 0), (0, GATE_RANK_PAD - GATE_RANK))),
                     cols(0, 2),
                     _rope_halves_first(cols(5, 6)), _rope_halves_first(cols(6, 7)),
                     cols(2, 4), cols(7, 9)]
        wup = bf(jnp.pad(w_gla_gate_up[i], ((0, GATE_RANK_PAD - GATE_RANK), (0, 0))))
        gout = jnp.concatenate([g_gla_out[i], g_ret_out[i]])
        next_weights = [w_ffn2_gate[i], w_ffn2_up[i], w_ffn2_down[i], w_ple_gate[i], w_ple_proj[i]]
        h, (wg2, wu2, wd2, wpg, wpp) = _call_a(
            h, cos, sin, consts, g_ffn1[i], bf(w_ffn1_gate[i]), bf(w_ffn1_up[i]), bf(w_ffn1_down[i]),
            g_mix[i], win_parts, wup, b_gla_gate[i], gout, bf(w_out[i]), next_weights)
        h = _call_b(h, p[i], g_ffn2[i], wg2, wu2, wd2, g_ple[i], wpg, wpp, g_final, final=(i == depth - 1))
    return h
```

```python
import functools

import jax
import jax.numpy as jnp
import numpy as np
from jax import lax
from jax.experimental import pallas as pl
from jax.experimental.pallas import tpu as pltpu

CHUNK = 64
HEADS = 4
DK = 64
DV = 128
QK = HEADS * DK
VW = HEADS * DV
GATE_RANK = 16
GATE_RANK_PAD = 128
GATE_NORM = 16.0
ROPE_BASE = 10000.0
EPS = 1e-6
LANES = 128
BF16_SUBLANES = 16
MXU_COLS = 256
FFN_PIECE_COLS = 4 * MXU_COLS

TILE_A = 512
CUM_ROWS = 256
TILE_B = 512
TAIL_GROUPS_B = 2
VMEM_LIMIT = 60 * 1024 * 1024

F32 = jnp.float32
BF16 = jnp.bfloat16


def _dot(a, b):
    return jnp.dot(a, b, preferred_element_type=F32)


def _rms(x, g):
    return x * lax.rsqrt(jnp.mean(x * x, axis=-1, keepdims=True) + EPS) * g


def _sigmoid(x):
    return 1.0 / (1.0 + jnp.exp(-x))


def _silu(x):
    return x * _sigmoid(x)


def _log_sigmoid(x):
    return jnp.minimum(x, 0.0) - jnp.log1p(jnp.exp(-jnp.abs(x)))


def _swiglu(xn, wg_ref, wu_ref, wd_ref):
    g = _dot(xn, wg_ref[...])
    u = _dot(xn, wu_ref[...])
    return _dot((_silu(g) * u).astype(BF16), wd_ref[...])


def _ffn_pieces(ff):
    return [slice(c, min(c + FFN_PIECE_COLS, ff)) for c in range(0, ff, FFN_PIECE_COLS)]


def _swiglu_up(xn, wg_ref, wu_ref, cols):
    return _dot(xn, wg_ref[:, cols]), _dot(xn, wu_ref[:, cols])


def _swiglu_down(gu, wd_ref, cols):
    g, u = gu
    return _dot((_silu(g) * u).astype(BF16), wd_ref[cols, :])


def _tile4(a):
    return jnp.concatenate([a, a, a, a], axis=0)


def _lanes2(a):
    return jnp.concatenate([a, a], axis=1)


def _lanes4(a):
    return jnp.concatenate([a, a, a, a], axis=1)


def _store_chunks_transposed(dst_ref, k):
    for c in range(k.shape[0] // CHUNK):
        blk = k[c * CHUNK:(c + 1) * CHUNK, :]
        dst_ref[c * QK:(c + 1) * QK, :] = jnp.concatenate([blk, blk], axis=0).T.astype(dst_ref.dtype)


def _call_a_kernel(x_ref, cos_ref, sin_ref, g1_ref, wg_ref, wu_ref, wd_ref, gmix_ref,
                   wlr_ref, wgqk_ref, wrq_ref, wrk_ref, wgvr_ref, wrvg_ref,
                   wup_ref, bgate_ref, gout_ref, wout_ref, ll_ref,
                   mkg_ref, mkr_ref, mv_ref, mwg_ref, mwr_ref, causal_ref, dmask_ref,
                   qdec_ref, kdec_ref, rdecay_ref,
                   out_ref,
                   st_g, st_r, o_s, h1_s, gr_s, rg_s, qf_s, qb_s, kf_s, kb_s, ks_s, gv_s, dec_s,
                   qr_s, qd_s, kr_s, kd_s, rv_s, *, tile, tiles_per_seq, flush):
    step = pl.program_id(0)
    carried = (o_s, h1_s, gr_s, rg_s, qf_s, qb_s, kf_s, kb_s, ks_s, gv_s, dec_s, qr_s, qd_s, kr_s, kd_s, rv_s)

    @pl.when(step == 0)
    def _():
        for ref in (st_g, st_r) + carried:
            ref[...] = jnp.zeros_like(ref)

    keep = lax.rem(jnp.maximum(step - 1, 0), tiles_per_seq) != 0
    st_g[...] = jnp.where(keep, st_g[...], 0.0)
    st_r[...] = jnp.where(keep, st_r[...], 0.0)

    half = DK // 2

    def gla_chunk(c):
        rows = pl.ds(c * CHUNK, CHUNK)
        qf = qf_s[rows, :]
        krows = pl.ds(c * QK, QK)
        s_f = _dot(qf, _lanes2(kf_s[krows, :]) * mkg_ref[...])
        s_b = _dot(qb_s[rows, :], _lanes2(kb_s[krows, :]) * mkg_ref[...])
        p = jnp.where(causal_ref[...] > 0.5, s_f, s_b).astype(BF16)
        v = gv_s[rows, :]
        st = st_g[...]
        w = _lanes4(st.astype(BF16)) * mwg_ref[...]
        o_s[rows, 0:VW] = _dot(p, _tile4(v) * mv_ref[...]) + _dot(qf, w)
        kt = ks_s[krows, :][:, :CHUNK]
        upd = [_dot(kt[h * DK:(h + 1) * DK, :], v[:, h * DV:(h + 1) * DV]) for h in range(HEADS)]
        st_g[...] = st * dec_s[krows, :] + jnp.concatenate(upd, axis=0)

    def ret_chunk(c):
        rows = pl.ds(c * CHUNK, CHUNK)
        s = _dot(qr_s[rows, :], _lanes2(kr_s[pl.ds(c * QK, QK), :]) * mkr_ref[...]) * dmask_ref[...]
        v = rv_s[rows, :]
        st = st_r[...]
        w = _lanes4(st.astype(BF16)) * mwr_ref[...]
        o_s[rows, VW:2 * VW] = _dot(s.astype(BF16), _tile4(v) * mv_ref[...]) + _dot(qd_s[rows, :], w)
        kt = kd_s[pl.ds(c * QK, QK), :][:, :CHUNK]
        upd = []
        for h in range(HEADS):
            k_h = jnp.concatenate([kt[h * half:(h + 1) * half, :],
                                   kt[LANES + h * half:LANES + (h + 1) * half, :]], axis=0)
            upd.append(_dot(k_h, v[:, h * DV:(h + 1) * DV]))
        upd = [u[:half] for u in upd] + [u[half:] for u in upd]
        st_r[...] = st * rdecay_ref[...] + jnp.concatenate(upd, axis=0)

    def mix_operand():
        gout = gout_ref[...]
        gates = (gr_s, rg_s)
        parts = []
        for h in range(2 * HEADS):
            oh = o_s[:, h * DV:(h + 1) * DV]
            gate = gates[h // HEADS][:, (h % HEADS) * DV:(h % HEADS + 1) * DV]
            parts.append((_rms(oh, gout[:, h * DV:(h + 1) * DV]) * _silu(gate)).astype(BF16))
        return jnp.concatenate(parts, axis=1)

    mixer_units = []
    for c in range(tile // CHUNK):
        mixer_units += [functools.partial(gla_chunk, c), functools.partial(ret_chunk, c)]

    if flush:
        for unit in mixer_units:
            unit()
        out_ref[0] = h1_s[...] + _dot(mix_operand(), wout_ref[...])
        return

    mixer_units[0]()
    x = x_ref[0]
    xn = _rms(x, g1_ref[...]).astype(BF16)
    pieces = _ffn_pieces(wd_ref.shape[0])
    n_pieces = len(pieces)
    spread = max(n_pieces - 1, 1)
    n_rest = len(mixer_units) - 1
    f = None
    mix = None
    gu = _swiglu_up(xn, wg_ref, wu_ref, pieces[0])
    for j in range(n_pieces):
        gu_next = _swiglu_up(xn, wg_ref, wu_ref, pieces[j + 1]) if j + 1 < n_pieces else None
        fj = _swiglu_down(gu, wd_ref, pieces[j])
        f = fj if f is None else f + fj
        gu = gu_next
        if j < spread:
            for u in range(1 + (j * n_rest) // spread, 1 + ((j + 1) * n_rest) // spread):
                mixer_units[u]()
        if j == spread - 1:
            mix = mix_operand()
    out_ref[0] = h1_s[...] + _dot(mix, wout_ref[...])
    h1 = x + 0.5 * f
    n = _rms(h1, gmix_ref[...]).astype(BF16)

    glr = _dot(n, wlr_ref[...])
    gqk = _dot(n, wgqk_ref[...])
    rq = _dot(n, wrq_ref[...])
    rk = _dot(n, wrk_ref[...])
    gq = gqk[:, :QK]
    gk = gqk[:, QK:]

    logit = _dot(glr.astype(BF16), wup_ref[...]) + bgate_ref[...]
    log_a = _log_sigmoid(logit) * (1.0 / GATE_NORM)
    la_hi = log_a.astype(BF16)
    la_lo = (log_a - la_hi.astype(F32)).astype(BF16)
    ll = ll_ref[...]
    cums, cls = [], []
    for r in range(tile // CUM_ROWS):
        blk = slice(r * CUM_ROWS, (r + 1) * CUM_ROWS)
        cc = _dot(ll, la_hi[blk]) + _dot(ll, la_lo[blk])
        cums.append(cc[:CUM_ROWS])
        cls.append(cc[CUM_ROWS:])
    cum = jnp.concatenate(cums, axis=0)
    cl = jnp.concatenate(cls, axis=0)
    e_pos = jnp.exp(cum)
    e_neg = jnp.exp(-cum)
    qs = gq * (DK ** -0.5)

    cos = cos_ref[...]
    sin = sin_ref[...]

    def rope(t):
        t1 = t[:, :LANES]
        t2 = t[:, LANES:]
        return jnp.concatenate([t1 * cos - t2 * sin, t2 * cos + t1 * sin], axis=1)

    qr = rope(rq) * (DK ** -0.5)
    kr = rope(rk)

    h1_s[...] = h1
    qf_s[...] = (qs * e_pos).astype(BF16)
    qb_s[...] = (qs * e_neg).astype(BF16)
    _store_chunks_transposed(kf_s, gk * e_neg)
    _store_chunks_transposed(kb_s, gk * e_pos)
    _store_chunks_transposed(ks_s, gk * jnp.exp(cl - cum))
    _store_chunks_transposed(dec_s, jnp.exp(cl))
    qr_s[...] = qr.astype(BF16)
    qd_s[...] = (qr * qdec_ref[...]).astype(BF16)
    _store_chunks_transposed(kr_s, kr)
    _store_chunks_transposed(kd_s, kr * kdec_ref[...])

    gvr = _dot(n, wgvr_ref[...])
    gv_s[...] = gvr[:, :VW].astype(BF16)
    gr_s[...] = gvr[:, VW:]
    rvg = _dot(n, wrvg_ref[...])
    rv_s[...] = rvg[:, :VW].astype(BF16)
    rg_s[...] = rvg[:, VW:]


def _call_b_kernel(h_ref, p_ref, g2_ref, wg_ref, wu_ref, wd_ref, gple_ref, wpg_ref, wpp_ref, gfin_ref,
                   out_ref, *, final):
    h = h_ref[0]
    f = _swiglu(_rms(h, g2_ref[...]).astype(BF16), wg_ref, wu_ref, wd_ref)
    pe = _dot(p_ref[0].astype(BF16), wpp_ref[...])
    h3 = h + 0.5 * f
    n3 = _rms(h3, gple_ref[...]).astype(BF16)
    group_rows = h.shape[0] // TAIL_GROUPS_B
    groups = [slice(k * group_rows, (k + 1) * group_rows) for k in range(TAIL_GROUPS_B)]
    gates = [_dot(n3[rows], wpg_ref[...]) for rows in groups]
    for rows, gate in zip(groups, gates):
        h4 = h3[rows] + _sigmoid(gate) * pe[rows]
        out_ref[0, rows, :] = _rms(h4, gfin_ref[...]) if final else h4


def _const_spec(shape):
    return pl.BlockSpec(shape, lambda *_: (0,) * len(shape), pipeline_mode=pl.Buffered(1))


def _mixer_constants(tile):
    r = np.arange(QK)
    kcol_g = r // DK
    kcol_r = (r % LANES) // (DK // 2)
    row_h = r // CHUNK
    mkg = (row_h[:, None] == kcol_g[None, :])
    mkr = (row_h[:, None] == kcol_r[None, :])
    mv = (row_h[:, None] == (np.arange(VW) // DV)[None, :])
    vrow_h = np.arange(VW) // DV
    mwg = (vrow_h[:, None] == kcol_g[None, :])
    mwr = (vrow_h[:, None] == kcol_r[None, :])
    i = np.arange(CHUNK)
    j = r % CHUNK
    causal = (i[:, None] >= j[None, :]).astype(np.float32)
    t = np.arange(CUM_ROWS)
    ll = np.concatenate([
        (t[:, None] // CHUNK == t[None, :] // CHUNK) & (t[None, :] <= t[:, None]),
        (t[:, None] // CHUNK == t[None, :] // CHUNK)], axis=0)

    log_gamma = np.log(1.0 - 2.0 ** (-5.0 - np.arange(HEADS, dtype=np.float64)))
    pos = np.arange(CHUNK, dtype=np.float64)
    dist = np.abs(pos[:, None] - pos[None, :])
    dmask = np.exp(log_gamma[:, None, None] * dist)
    dmask = dmask.transpose(1, 0, 2).reshape(CHUNK, QK)
    lg_col = log_gamma[kcol_r]
    qdec = np.exp(lg_col[None, :] * (pos + 1.0)[:, None])
    kdec = np.exp(lg_col[None, :] * (CHUNK - 1 - pos)[:, None])
    rdecay = np.tile(np.exp(lg_col * CHUNK)[:, None], (1, DV))
    reps = tile // CHUNK
    as_bf = lambda m: jnp.asarray(m.astype(np.float32), dtype=BF16)
    as_f32 = lambda m: jnp.asarray(m.astype(np.float32))
    return dict(ll=as_bf(ll), mkg=as_bf(mkg.T), mkr=as_bf(mkr.T), mv=as_bf(mv), mwg=as_bf(mwg.T), mwr=as_bf(mwr.T),
                causal=as_f32(causal), dmask=as_f32(dmask), qdec=as_f32(np.tile(qdec, (reps, 1))),
                kdec=as_f32(np.tile(kdec, (reps, 1))), rdecay=as_f32(rdecay))


def _rope_tables(seq):
    half = DK // 2
    inv = ROPE_BASE ** (-np.arange(half, dtype=np.float64) / half)
    ang = np.arange(seq, dtype=np.float64)[:, None] * inv[None, :]
    table = lambda t: jnp.asarray(np.tile(t, (1, HEADS)).astype(np.float32))
    return table(np.cos(ang)), table(np.sin(ang))


def _rope_halves_first(w):
    rows = w.shape[0]
    return w.reshape(rows, HEADS, 2, DK // 2).transpose(0, 2, 1, 3).reshape(rows, QK)


def _call_a_with_casts_kernel(*refs, n_in, n_cast, **kw):
    ins, cast_in = refs[:n_in], refs[n_in:n_in + n_cast]
    out_ref, cast_out = refs[n_in + n_cast], refs[n_in + n_cast + 1:n_in + 2 * n_cast + 1]
    scratch = refs[n_in + 2 * n_cast + 1:]
    for src, dst in zip(cast_in, cast_out):
        dst[...] = src[...].astype(BF16)
    step, last = pl.program_id(0), pl.num_programs(0) - 1
    pl.when(step < last)(functools.partial(_call_a_kernel, *ins, out_ref, *scratch, flush=False, **kw))
    pl.when(step == last)(functools.partial(_call_a_kernel, *ins, out_ref, *scratch, flush=True, **kw))


def _cast_block_rows(rows, max_steps):
    for block in range(BF16_SUBLANES, rows + 1, BF16_SUBLANES):
        if rows % block == 0 and rows // block <= max_steps:
            return block
    raise ValueError(f"no row block for {rows} rows in {max_steps} steps")


def _call_a(h, cos, sin, consts, g1, wg, wu, wd, gmix, win_parts, wup, bgate, gout, wout, cast_weights):
    b, s, d = h.shape
    tile = TILE_A
    tps = s // tile
    n_tiles = b * tps
    row = lambda a: a.reshape(1, -1)
    args = [h, cos, sin, row(g1), wg, wu, wd, row(gmix), *win_parts, wup, row(bgate), row(gout), wout,
            consts["ll"], consts["mkg"], consts["mkr"], consts["mv"], consts["mwg"], consts["mwr"],
            consts["causal"], consts["dmask"], consts["qdec"], consts["kdec"], consts["rdecay"]]
    proj_tile = lambda st: jnp.minimum(st, n_tiles - 1)
    mix_tile = lambda st: jnp.maximum(st - 1, 0)
    in_specs = [pl.BlockSpec((1, tile, d), lambda st: (proj_tile(st) // tps, proj_tile(st) % tps, 0)),
                pl.BlockSpec((tile, LANES), lambda st: (proj_tile(st) % tps, 0)),
                pl.BlockSpec((tile, LANES), lambda st: (proj_tile(st) % tps, 0))]
    in_specs += [_const_spec(a.shape) for a in args[3:]]
    bf = lambda *shape: pltpu.VMEM(shape, BF16)
    f32 = lambda *shape: pltpu.VMEM(shape, F32)
    kt = bf(tile // CHUNK * QK, 2 * CHUNK)
    scratch = [f32(QK, DV), f32(QK, DV), f32(tile, 2 * VW), f32(tile, d), f32(tile, VW), f32(tile, VW),
               bf(tile, QK), bf(tile, QK), kt, kt, kt, bf(tile, VW),
               f32(tile // CHUNK * QK, 2 * CHUNK),
               bf(tile, QK), bf(tile, QK), kt, kt, bf(tile, VW)]
    out_specs = [pl.BlockSpec((1, tile, d), lambda st: (mix_tile(st) // tps, mix_tile(st) % tps, 0))]
    out_shape = [jax.ShapeDtypeStruct((b, s, d), F32)]
    n_in = len(args)
    for w in cast_weights:
        rows, cols = w.shape
        block = _cast_block_rows(rows, n_tiles)
        spec = pl.BlockSpec((block, cols), functools.partial(lambda st, last: (jnp.minimum(st, last), 0),
                                                             last=rows // block - 1))
        in_specs.append(spec)
        out_specs.append(spec)
        out_shape.append(jax.ShapeDtypeStruct(w.shape, BF16))
    outs = pl.pallas_call(
        functools.partial(_call_a_with_casts_kernel, n_in=n_in, n_cast=len(cast_weights),
                          tile=tile, tiles_per_seq=tps),
        grid=(n_tiles + 1,),
        in_specs=in_specs,
        out_specs=out_specs,
        out_shape=out_shape,
        scratch_shapes=scratch,
        compiler_params=pltpu.CompilerParams(dimension_semantics=("arbitrary",),
                                             vmem_limit_bytes=VMEM_LIMIT),
        name="ffn_a_mixers",
    )(*args, *cast_weights)
    return outs[0], outs[1:]


def _call_b(h, p, g2, wg, wu, wd, gple, wpg, wpp, gfin, final):
    b, s, d = h.shape
    tile = TILE_B
    row = lambda a: a.reshape(1, -1)
    args = [h, p, row(g2), wg, wu, wd, row(gple), wpg, wpp, row(gfin)]
    in_specs = [pl.BlockSpec((1, tile, d), lambda bi, si: (bi, si, 0)),
                pl.BlockSpec((1, tile, p.shape[-1]), lambda bi, si: (bi, si, 0))]
    in_specs += [_const_spec(a.shape) for a in args[2:]]
    return pl.pallas_call(
        functools.partial(_call_b_kernel, final=final),
        grid=(b, s // tile),
        in_specs=in_specs,
        out_specs=pl.BlockSpec((1, tile, d), lambda bi, si: (bi, si, 0)),
        out_shape=jax.ShapeDtypeStruct((b, s, d), F32),
        compiler_params=pltpu.CompilerParams(dimension_semantics=("arbitrary", "arbitrary"),
                                             vmem_limit_bytes=VMEM_LIMIT),
        name="ffn_b_embed",
    )(*args)


def kernel(x, p, g_ffn1, w_ffn1_gate, w_ffn1_up, w_ffn1_down, g_mix, w_in, w_gla_gate_up, b_gla_gate, g_gla_out, g_ret_out, w_out, g_ffn2, w_ffn2_gate, w_ffn2_up, w_ffn2_down, g_ple, w_ple_gate, w_ple_proj, g_final):
    depth = p.shape[0]
    seq = x.shape[1]
    assert seq % TILE_A == 0 and seq % TILE_B == 0 and TILE_A % CHUNK == 0
    assert w_ffn1_gate.shape[2] % MXU_COLS == 0
    consts = _mixer_constants(TILE_A)
    cos, sin = _rope_tables(seq)
    sizes = [QK, QK, VW, VW, GATE_RANK, QK, QK, VW, VW]
    offs = np.concatenate([[0], np.cumsum(sizes)])
    bf = lambda a: a.astype(BF16)
    h = x
    for i in range(depth):
        wi = w_in[i]
        cols = lambda j0, j1: bf(wi[:, offs[j0]:offs[j1]])
        win_parts = [jnp.pad(cols(4, 5), ((0, 0), (0, GATE_RANK_PAD - GATE_RANK))),
                     cols(0, 2),
                     _rope_halves_first(cols(5, 6)), _rope_halves_first(cols(6, 7)),
                     cols(2, 4), cols(7, 9)]
        wup = bf(jnp.pad(w_gla_gate_up[i], ((0, GATE_RANK_PAD - GATE_RANK), (0, 0))))
        gout = jnp.concatenate([g_gla_out[i], g_ret_out[i]])
        next_weights = [w_ffn2_gate[i], w_ffn2_up[i], w_ffn2_down[i], w_ple_gate[i], w_ple_proj[i]]
        h, (wg2, wu2, wd2, wpg, wpp) = _call_a(
            h, cos, sin, consts, g_ffn1[i], bf(w_ffn1_gate[i]), bf(w_ffn1_up[i]), bf(w_ffn1_down[i]),
            g_mix[i], win_parts, wup, b_gla_gate[i], gout, bf(w_out[i]), next_weights)
        h = _call_b(h, p[i], g_ffn2[i], wg2, wu2, wd2, g_ple[i], wpg, wpp, g_final, final=(i == depth - 1))
    return h
```

```python
import functools

import jax
import jax.numpy as jnp
import numpy as np
from jax import lax
from jax.experimental import pallas as pl
from jax.experimental.pallas import tpu as pltpu

CHUNK = 64
HEADS = 4
DK = 64
DV = 128
QK = HEADS * DK
VW = HEADS * DV
GATE_RANK = 16
GATE_RANK_PAD = 128
GATE_NORM = 16.0
ROPE_BASE = 10000.0
EPS = 1e-6
LANES = 128
BF16_SUBLANES = 16
MXU_COLS = 256
FFN_PIECE_COLS = 4 * MXU_COLS

TILE_A = 512
CUM_ROWS = 256
TILE_B = 512
TAIL_GROUPS_B = 2
VMEM_LIMIT = 60 * 1024 * 1024

F32 = jnp.float32
BF16 = jnp.bfloat16


def _dot(a, b):
    return jnp.dot(a, b, preferred_element_type=F32)


def _rms(x, g):
    return x * lax.rsqrt(jnp.mean(x * x, axis=-1, keepdims=True) + EPS) * g


def _sigmoid(x):
    return 1.0 / (1.0 + jnp.exp(-x))


def _silu(x):
    return x * _sigmoid(x)


def _log_sigmoid(x):
    return jnp.minimum(x, 0.0) - jnp.log1p(jnp.exp(-jnp.abs(x)))


def _swiglu(xn, wg_ref, wu_ref, wd_ref):
    g = _dot(xn, wg_ref[...])
    u = _dot(xn, wu_ref[...])
    return _dot((_silu(g) * u).astype(BF16), wd_ref[...])


def _ffn_pieces(ff):
    return [slice(c, min(c + FFN_PIECE_COLS, ff)) for c in range(0, ff, FFN_PIECE_COLS)]


def _swiglu_up(xn, wg_ref, wu_ref, cols):
    return _dot(xn, wg_ref[:, cols]), _dot(xn, wu_ref[:, cols])


def _swiglu_down(gu, wd_ref, cols):
    g, u = gu
    return _dot((_silu(g) * u).astype(BF16), wd_ref[cols, :])


def _tile4(a):
    return jnp.concatenate([a, a, a, a], axis=0)


def _lanes2(a):
    return jnp.concatenate([a, a], axis=1)


def _lanes4(a):
    return jnp.concatenate([a, a, a, a], axis=1)


def _store_chunks_transposed(dst_ref, k):
    for c in range(k.shape[0] // CHUNK):
        blk = k[c * CHUNK:(c + 1) * CHUNK, :]
        dst_ref[c * QK:(c + 1) * QK, :] = jnp.concatenate([blk, blk], axis=0).T.astype(dst_ref.dtype)


def _call_a_kernel(x_ref, cos_ref, sin_ref, g1_ref, wg_ref, wu_ref, wd_ref, gmix_ref,
                   wlr_ref, wgqk_ref, wrq_ref, wrk_ref, wgvr_ref, wrvg_ref,
                   wup_ref, bgate_ref, gout_ref, wout_ref, ll_ref,
                   mkg_ref, mkr_ref, mv_ref, mwg_ref, mwr_ref, causal_ref, dmask_ref,
                   qdec_ref, kdec_ref, rdecay_ref,
                   out_ref,
                   st_g, st_r, o_s, h1_s, gr_s, rg_s, qf_s, qb_s, kf_s, kb_s, ks_s, gv_s, dec_s,
                   qr_s, qd_s, kr_s, kd_s, rv_s, *, tile, tiles_per_seq, flush):
    step = pl.program_id(0)
    carried = (o_s, h1_s, gr_s, rg_s, qf_s, qb_s, kf_s, kb_s, ks_s, gv_s, dec_s, qr_s, qd_s, kr_s, kd_s, rv_s)

    @pl.when(step == 0)
    def _():
        for ref in (st_g, st_r) + carried:
            ref[...] = jnp.zeros_like(ref)

    keep = lax.rem(jnp.maximum(step - 1, 0), tiles_per_seq) != 0
    st_g[...] = jnp.where(keep, st_g[...], 0.0)
    st_r[...] = jnp.where(keep, st_r[...], 0.0)

    half = DK // 2

    def gla_chunk(c):
        rows = pl.ds(c * CHUNK, CHUNK)
        qf = qf_s[rows, :]
        krows = pl.ds(c * QK, QK)
        s_f = _dot(qf, _lanes2(kf_s[krows, :]) * mkg_ref[...])
        s_b = _dot(qb_s[rows, :], _lanes2(kb_s[krows, :]) * mkg_ref[...])
        p = jnp.where(causal_ref[...] > 0.5, s_f, s_b).astype(BF16)
        v = gv_s[rows, :]
        st = st_g[...]
        w = _lanes4(st.astype(BF16)) * mwg_ref[...]
        o_s[rows, 0:VW] = _dot(p, _tile4(v) * mv_ref[...]) + _dot(qf, w)
        kt = ks_s[krows, :][:, :CHUNK]
        kv = _dot(kt, v)
        upd = [kv[h * DK:(h + 1) * DK, h * DV:(h + 1) * DV] for h in range(HEADS)]
        st_g[...] = st * dec_s[krows, :] + jnp.concatenate(upd, axis=0)

    def ret_chunk(c):
        rows = pl.ds(c * CHUNK, CHUNK)
        s = _dot(qr_s[rows, :], _lanes2(kr_s[pl.ds(c * QK, QK), :]) * mkr_ref[...]) * dmask_ref[...]
        v = rv_s[rows, :]
        st = st_r[...]
        w = _lanes4(st.astype(BF16)) * mwr_ref[...]
        o_s[rows, VW:2 * VW] = _dot(s.astype(BF16), _tile4(v) * mv_ref[...]) + _dot(qd_s[rows, :], w)
        kt = kd_s[pl.ds(c * QK, QK), :][:, :CHUNK]
        kv = _dot(kt, v)
        upd = [kv[r * LANES + h * half:r * LANES + (h + 1) * half, h * DV:(h + 1) * DV]
               for r in range(2) for h in range(HEADS)]
        st_r[...] = st * rdecay_ref[...] + jnp.concatenate(upd, axis=0)

    def mix_operand():
        gout = gout_ref[...]
        gates = (gr_s, rg_s)
        parts = []
        for h in range(2 * HEADS):
            oh = o_s[:, h * DV:(h + 1) * DV]
            gate = gates[h // HEADS][:, (h % HEADS) * DV:(h % HEADS + 1) * DV]
            parts.append((_rms(oh, gout[:, h * DV:(h + 1) * DV]) * _silu(gate)).astype(BF16))
        return jnp.concatenate(parts, axis=1)

    mixer_units = []
    for c in range(tile // CHUNK):
        mixer_units += [functools.partial(gla_chunk, c), functools.partial(ret_chunk, c)]

    if flush:
        for unit in mixer_units:
            unit()
        out_ref[0] = h1_s[...] + _dot(mix_operand(), wout_ref[...])
        return

    mixer_units[0]()
    x = x_ref[0]
    xn = _rms(x, g1_ref[...]).astype(BF16)
    pieces = _ffn_pieces(wd_ref.shape[0])
    n_pieces = len(pieces)
    spread = max(n_pieces - 1, 1)
    n_rest = len(mixer_units) - 1
    f = None
    mix = None
    gu = _swiglu_up(xn, wg_ref, wu_ref, pieces[0])
    for j in range(n_pieces):
        gu_next = _swiglu_up(xn, wg_ref, wu_ref, pieces[j + 1]) if j + 1 < n_pieces else None
        fj = _swiglu_down(gu, wd_ref, pieces[j])
        f = fj if f is None else f + fj
        gu = gu_next
        if j < spread:
            for u in range(1 + (j * n_rest) // spread, 1 + ((j + 1) * n_rest) // spread):
                mixer_units[u]()
        if j == spread - 1:
            mix = mix_operand()
    out_ref[0] = h1_s[...] + _dot(mix, wout_ref[...])
    h1 = x + 0.5 * f
    n = _rms(h1, gmix_ref[...]).astype(BF16)

    glr = _dot(n, wlr_ref[...])
    gqk = _dot(n, wgqk_ref[...])
    rq = _dot(n, wrq_ref[...])
    rk = _dot(n, wrk_ref[...])
    gq = gqk[:, :QK]
    gk = gqk[:, QK:]

    logit = _dot(glr.astype(BF16), wup_ref[...]) + bgate_ref[...]
    log_a = _log_sigmoid(logit) * (1.0 / GATE_NORM)
    la_hi = log_a.astype(BF16)
    la_lo = (log_a - la_hi.astype(F32)).astype(BF16)
    ll = ll_ref[...]
    cums, cls = [], []
    for r in range(tile // CUM_ROWS):
        blk = slice(r * CUM_ROWS, (r + 1) * CUM_ROWS)
        cc = _dot(ll, la_hi[blk]) + _dot(ll, la_lo[blk])
        cums.append(cc[:CUM_ROWS])
        cls.append(cc[CUM_ROWS:])
    cum = jnp.concatenate(cums, axis=0)
    cl = jnp.concatenate(cls, axis=0)
    e_pos = jnp.exp(cum)
    e_neg = jnp.exp(-cum)
    qs = gq * (DK ** -0.5)

    cos = cos_ref[...]
    sin = sin_ref[...]

    def rope(t):
        t1 = t[:, :LANES]
        t2 = t[:, LANES:]
        return jnp.concatenate([t1 * cos - t2 * sin, t2 * cos + t1 * sin], axis=1)

    qr = rope(rq) * (DK ** -0.5)
    kr = rope(rk)

    h1_s[...] = h1
    qf_s[...] = (qs * e_pos).astype(BF16)
    qb_s[...] = (qs * e_neg).astype(BF16)
    _store_chunks_transposed(kf_s, gk * e_neg)
    _store_chunks_transposed(kb_s, gk * e_pos)
    _store_chunks_transposed(ks_s, gk * jnp.exp(cl - cum))
    _store_chunks_transposed(dec_s, jnp.exp(cl))
    qr_s[...] = qr.astype(BF16)
    qd_s[...] = (qr * qdec_ref[...]).astype(BF16)
    _store_chunks_transposed(kr_s, kr)
    _store_chunks_transposed(kd_s, kr * kdec_ref[...])

    gvr = _dot(n, wgvr_ref[...])
    gv_s[...] = gvr[:, :VW].astype(BF16)
    gr_s[...] = gvr[:, VW:]
    rvg = _dot(n, wrvg_ref[...])
    rv_s[...] = rvg[:, :VW].astype(BF16)
    rg_s[...] = rvg[:, VW:]


def _call_b_kernel(h_ref, p_ref, g2_ref, wg_ref, wu_ref, wd_ref, gple_ref, wpg_ref, wpp_ref, gfin_ref,
                   out_ref, *, final):
    h = h_ref[0]
    f = _swiglu(_rms(h, g2_ref[...]).astype(BF16), wg_ref, wu_ref, wd_ref)
    pe = _dot(p_ref[0].astype(BF16), wpp_ref[...])
    h3 = h + 0.5 * f
    n3 = _rms(h3, gple_ref[...]).astype(BF16)
    group_rows = h.shape[0] // TAIL_GROUPS_B
    groups = [slice(k * group_rows, (k + 1) * group_rows) for k in range(TAIL_GROUPS_B)]
    gates = [_dot(n3[rows], wpg_ref[...]) for rows in groups]
    for rows, gate in zip(groups, gates):
        h4 = h3[rows] + _sigmoid(gate) * pe[rows]
        out_ref[0, rows, :] = _rms(h4, gfin_ref[...]) if final else h4


def _const_spec(shape):
    return pl.BlockSpec(shape, lambda *_: (0,) * len(shape), pipeline_mode=pl.Buffered(1))


def _mixer_constants(tile):
    r = np.arange(QK)
    kcol_g = r // DK
    kcol_r = (r % LANES) // (DK // 2)
    row_h = r // CHUNK
    mkg = (row_h[:, None] == kcol_g[None, :])
    mkr = (row_h[:, None] == kcol_r[None, :])
    mv = (row_h[:, None] == (np.arange(VW) // DV)[None, :])
    vrow_h = np.arange(VW) // DV
    mwg = (vrow_h[:, None] == kcol_g[None, :])
    mwr = (vrow_h[:, None] == kcol_r[None, :])
    i = np.arange(CHUNK)
    j = r % CHUNK
    causal = (i[:, None] >= j[None, :]).astype(np.float32)
    t = np.arange(CUM_ROWS)
    ll = np.concatenate([
        (t[:, None] // CHUNK == t[None, :] // CHUNK) & (t[None, :] <= t[:, None]),
        (t[:, None] // CHUNK == t[None, :] // CHUNK)], axis=0)

    log_gamma = np.log(1.0 - 2.0 ** (-5.0 - np.arange(HEADS, dtype=np.float64)))
    pos = np.arange(CHUNK, dtype=np.float64)
    dist = np.abs(pos[:, None] - pos[None, :])
    dmask = np.exp(log_gamma[:, None, None] * dist)
    dmask = dmask.transpose(1, 0, 2).reshape(CHUNK, QK)
    lg_col = log_gamma[kcol_r]
    qdec = np.exp(lg_col[None, :] * (pos + 1.0)[:, None])
    kdec = np.exp(lg_col[None, :] * (CHUNK - 1 - pos)[:, None])
    rdecay = np.tile(np.exp(lg_col * CHUNK)[:, None], (1, DV))
    reps = tile // CHUNK
    as_bf = lambda m: jnp.asarray(m.astype(np.float32), dtype=BF16)
    as_f32 = lambda m: jnp.asarray(m.astype(np.float32))
    return dict(ll=as_bf(ll), mkg=as_bf(mkg.T), mkr=as_bf(mkr.T), mv=as_bf(mv), mwg=as_bf(mwg.T), mwr=as_bf(mwr.T),
                causal=as_f32(causal), dmask=as_f32(dmask), qdec=as_f32(np.tile(qdec, (reps, 1))),
                kdec=as_f32(np.tile(kdec, (reps, 1))), rdecay=as_f32(rdecay))


def _rope_tables(seq):
    half = DK // 2
    inv = ROPE_BASE ** (-np.arange(half, dtype=np.float64) / half)
    ang = np.arange(seq, dtype=np.float64)[:, None] * inv[None, :]
    table = lambda t: jnp.asarray(np.tile(t, (1, HEADS)).astype(np.float32))
    return table(np.cos(ang)), table(np.sin(ang))


def _rope_halves_first(w):
    rows = w.shape[0]
    return w.reshape(rows, HEADS, 2, DK // 2).transpose(0, 2, 1, 3).reshape(rows, QK)


def _call_a_with_casts_kernel(*refs, n_in, n_cast, **kw):
    ins, cast_in = refs[:n_in], refs[n_in:n_in + n_cast]
    out_ref, cast_out = refs[n_in + n_cast], refs[n_in + n_cast + 1:n_in + 2 * n_cast + 1]
    scratch = refs[n_in + 2 * n_cast + 1:]
    for src, dst in zip(cast_in, cast_out):
        dst[...] = src[...].astype(BF16)
    step, last = pl.program_id(0), pl.num_programs(0) - 1
    pl.when(step < last)(functools.partial(_call_a_kernel, *ins, out_ref, *scratch, flush=False, **kw))
    pl.when(step == last)(functools.partial(_call_a_kernel, *ins, out_ref, *scratch, flush=True, **kw))


def _cast_block_rows(rows, max_steps):
    for block in range(BF16_SUBLANES, rows + 1, BF16_SUBLANES):
        if rows % block == 0 and rows // block <= max_steps:
            return block
    raise ValueError(f"no row block for {rows} rows in {max_steps} steps")


def _call_a(h, cos, sin, consts, g1, wg, wu, wd, gmix, win_parts, wup, bgate, gout, wout, cast_weights):
    b, s, d = h.shape
    tile = TILE_A
    tps = s // tile
    n_tiles = b * tps
    row = lambda a: a.reshape(1, -1)
    args = [h, cos, sin, row(g1), wg, wu, wd, row(gmix), *win_parts, wup, row(bgate), row(gout), wout,
            consts["ll"], consts["mkg"], consts["mkr"], consts["mv"], consts["mwg"], consts["mwr"],
            consts["causal"], consts["dmask"], consts["qdec"], consts["kdec"], consts["rdecay"]]
    proj_tile = lambda st: jnp.minimum(st, n_tiles - 1)
    mix_tile = lambda st: jnp.maximum(st - 1, 0)
    in_specs = [pl.BlockSpec((1, tile, d), lambda st: (proj_tile(st) // tps, proj_tile(st) % tps, 0)),
                pl.BlockSpec((tile, LANES), lambda st: (proj_tile(st) % tps, 0)),
                pl.BlockSpec((tile, LANES), lambda st: (proj_tile(st) % tps, 0))]
    in_specs += [_const_spec(a.shape) for a in args[3:]]
    bf = lambda *shape: pltpu.VMEM(shape, BF16)
    f32 = lambda *shape: pltpu.VMEM(shape, F32)
    kt = bf(tile // CHUNK * QK, 2 * CHUNK)
    scratch = [f32(QK, DV), f32(QK, DV), f32(tile, 2 * VW), f32(tile, d), f32(tile, VW), f32(tile, VW),
               bf(tile, QK), bf(tile, QK), kt, kt, kt, bf(tile, VW),
               f32(tile // CHUNK * QK, 2 * CHUNK),
               bf(tile, QK), bf(tile, QK), kt, kt, bf(tile, VW)]
    out_specs = [pl.BlockSpec((1, tile, d), lambda st: (mix_tile(st) // tps, mix_tile(st) % tps, 0))]
    out_shape = [jax.ShapeDtypeStruct((b, s, d), F32)]
    n_in = len(args)
    for w in cast_weights:
        rows, cols = w.shape
        block = _cast_block_rows(rows, n_tiles)
        spec = pl.BlockSpec((block, cols), functools.partial(lambda st, last: (jnp.minimum(st, last), 0),
                                                             last=rows // block - 1))
        in_specs.append(spec)
        out_specs.append(spec)
        out_shape.append(jax.ShapeDtypeStruct(w.shape, BF16))
    outs = pl.pallas_call(
        functools.partial(_call_a_with_casts_kernel, n_in=n_in, n_cast=len(cast_weights),
                          tile=tile, tiles_per_seq=tps),
        grid=(n_tiles + 1,),
        in_specs=in_specs,
        out_specs=out_specs,
        out_shape=out_shape,
        scratch_shapes=scratch,
        compiler_params=pltpu.CompilerParams(dimension_semantics=("arbitrary",),
                                             vmem_limit_bytes=VMEM_LIMIT),
        name="ffn_a_mixers",
    )(*args, *cast_weights)
    return outs[0], outs[1:]


def _call_b(h, p, g2, wg, wu, wd, gple, wpg, wpp, gfin, final):
    b, s, d = h.shape
    tile = TILE_B
    row = lambda a: a.reshape(1, -1)
    args = [h, p, row(g2), wg, wu, wd, row(gple), wpg, wpp, row(gfin)]
    in_specs = [pl.BlockSpec((1, tile, d), lambda bi, si: (bi, si, 0)),
                pl.BlockSpec((1, tile, p.shape[-1]), lambda bi, si: (bi, si, 0))]
    in_specs += [_const_spec(a.shape) for a in args[2:]]
    return pl.pallas_call(
        functools.partial(_call_b_kernel, final=final),
        grid=(b, s // tile),
        in_specs=in_specs,
        out_specs=pl.BlockSpec((1, tile, d), lambda bi, si: (bi, si, 0)),
        out_shape=jax.ShapeDtypeStruct((b, s, d), F32),
        compiler_params=pltpu.CompilerParams(dimension_semantics=("arbitrary", "arbitrary"),
                                             vmem_limit_bytes=VMEM_LIMIT),
        name="ffn_b_embed",
    )(*args)


def kernel(x, p, g_ffn1, w_ffn1_gate, w_ffn1_up, w_ffn1_down, g_mix, w_in, w_gla_gate_up, b_gla_gate, g_gla_out, g_ret_out, w_out, g_ffn2, w_ffn2_gate, w_ffn2_up, w_ffn2_down, g_ple, w_ple_gate, w_ple_proj, g_final):
    depth = p.shape[0]
    seq = x.shape[1]
    assert seq % TILE_A == 0 and seq % TILE_B == 0 and TILE_A % CHUNK == 0
    assert w_ffn1_gate.shape[2] % MXU_COLS == 0
    consts = _mixer_constants(TILE_A)
    cos, sin = _rope_tables(seq)
    sizes = [QK, QK, VW, VW, GATE_RANK, QK, QK, VW, VW]
    offs = np.concatenate([[0], np.cumsum(sizes)])
    bf = lambda a: a.astype(BF16)
    h = x
    for i in range(depth):
        wi = w_in[i]
        cols = lambda j0, j1: bf(wi[:, offs[j0]:offs[j1]])
        win_parts = [jnp.pad(cols(4, 5), ((0, 0), (0, GATE_RANK_PAD - GATE_RANK))),
                     cols(0, 2),
                     _rope_halves_first(cols(5, 6)), _rope_halves_first(cols(6, 7)),
                     cols(2, 4), cols(7, 9)]
        wup = bf(jnp.pad(w_gla_gate_up[i], ((0, GATE_RANK_PAD - GATE_RANK), (0, 0))))
        gout = jnp.concatenate([g_gla_out[i], g_ret_out[i]])
        next_weights = [w_ffn2_gate[i], w_ffn2_up[i], w_ffn2_down[i], w_ple_gate[i], w_ple_proj[i]]
        h, (wg2, wu2, wd2, wpg, wpp) = _call_a(
            h, cos, sin, consts, g_ffn1[i], bf(w_ffn1_gate[i]), bf(w_ffn1_up[i]), bf(w_ffn1_down[i]),
            g_mix[i], win_parts, wup, b_gla_gate[i], gout, bf(w_out[i]), next_weights)
        h = _call_b(h, p[i], g_ffn2[i], wg2, wu2, wd2, g_ple[i], wpg, wpp, g_final, final=(i == depth - 1))
    return h
```

```python
import functools

import jax
import jax.numpy as jnp
import numpy as np
from jax import lax
from jax.experimental import pallas as pl
from jax.experimental.pallas import tpu as pltpu

CHUNK = 64
HEADS = 4
DK = 64
DV = 128
QK = HEADS * DK
VW = HEADS * DV
GATE_RANK = 16
GATE_RANK_PAD = 128
GATE_NORM = 16.0
ROPE_BASE = 10000.0
EPS = 1e-6
LANES = 128
BF16_SUBLANES = 16
MXU_COLS = 256
FFN_PIECE_COLS = 6 * MXU_COLS

TILE_A = 512
CUM_ROWS = 256
TILE_B = 512
TAIL_GROUPS_B = 2
VMEM_LIMIT = 60 * 1024 * 1024

F32 = jnp.float32
BF16 = jnp.bfloat16


def _dot(a, b):
    return jnp.dot(a, b, preferred_element_type=F32)


def _rms(x, g):
    return x * lax.rsqrt(jnp.mean(x * x, axis=-1, keepdims=True) + EPS) * g


def _sigmoid(x):
    return 1.0 / (1.0 + jnp.exp(-x))


def _silu(x):
    return x * _sigmoid(x)


def _log_sigmoid(x):
    return jnp.minimum(x, 0.0) - jnp.log1p(jnp.exp(-jnp.abs(x)))


def _swiglu(xn, wg_ref, wu_ref, wd_ref):
    g = _dot(xn, wg_ref[...])
    u = _dot(xn, wu_ref[...])
    return _dot((_silu(g) * u).astype(BF16), wd_ref[...])


def _ffn_pieces(ff):
    return [slice(c, min(c + FFN_PIECE_COLS, ff)) for c in range(0, ff, FFN_PIECE_COLS)]


def _swiglu_up(xn, wg_ref, wu_ref, cols):
    return _dot(xn, wg_ref[:, cols]), _dot(xn, wu_ref[:, cols])


def _swiglu_down(gu, wd_ref, cols):
    g, u = gu
    return _dot((_silu(g) * u).astype(BF16), wd_ref[cols, :])


def _tile4(a):
    return jnp.concatenate([a, a, a, a], axis=0)


def _lanes2(a):
    return jnp.concatenate([a, a], axis=1)


def _lanes4(a):
    return jnp.concatenate([a, a, a, a], axis=1)


def _store_chunks_transposed(dst_ref, k):
    for c in range(k.shape[0] // CHUNK):
        blk = k[c * CHUNK:(c + 1) * CHUNK, :]
        dst_ref[c * QK:(c + 1) * QK, :] = jnp.concatenate([blk, blk], axis=0).T.astype(dst_ref.dtype)


def _call_a_kernel(x_ref, cos_ref, sin_ref, g1_ref, wg_ref, wu_ref, wd_ref, gmix_ref,
                   wlr_ref, wgqk_ref, wrq_ref, wrk_ref, wgvr_ref, wrvg_ref,
                   wup_ref, bgate_ref, gout_ref, wout_ref, ll_ref,
                   mkg_ref, mkr_ref, mv_ref, mwg_ref, mwr_ref, causal_ref, dmask_ref,
                   qdec_ref, kdec_ref, rdecay_ref,
                   out_ref,
                   st_g, st_r, o_s, h1_s, gr_s, rg_s, qf_s, qb_s, kf_s, kb_s, ks_s, gv_s, dec_s,
                   qr_s, qd_s, kr_s, kd_s, rv_s, *, tile, tiles_per_seq, flush):
    step = pl.program_id(0)
    carried = (o_s, h1_s, gr_s, rg_s, qf_s, qb_s, kf_s, kb_s, ks_s, gv_s, dec_s, qr_s, qd_s, kr_s, kd_s, rv_s)

    @pl.when(step == 0)
    def _():
        for ref in (st_g, st_r) + carried:
            ref[...] = jnp.zeros_like(ref)

    keep = lax.rem(jnp.maximum(step - 1, 0), tiles_per_seq) != 0
    st_g[...] = jnp.where(keep, st_g[...], 0.0)
    st_r[...] = jnp.where(keep, st_r[...], 0.0)

    half = DK // 2

    def gla_chunk(c):
        rows = pl.ds(c * CHUNK, CHUNK)
        qf = qf_s[rows, :]
        krows = pl.ds(c * QK, QK)
        s_f = _dot(qf, _lanes2(kf_s[krows, :]) * mkg_ref[...])
        s_b = _dot(qb_s[rows, :], _lanes2(kb_s[krows, :]) * mkg_ref[...])
        p = jnp.where(causal_ref[...] > 0.5, s_f, s_b).astype(BF16)
        v = gv_s[rows, :]
        st = st_g[...]
        w = _lanes4(st.astype(BF16)) * mwg_ref[...]
        o_s[rows, 0:VW] = _dot(p, _tile4(v) * mv_ref[...]) + _dot(qf, w)
        kt = ks_s[krows, :][:, :CHUNK]
        upd = [_dot(kt[h * DK:(h + 1) * DK, :], v[:, h * DV:(h + 1) * DV]) for h in range(HEADS)]
        st_g[...] = st * dec_s[krows, :] + jnp.concatenate(upd, axis=0)

    def ret_chunk(c):
        rows = pl.ds(c * CHUNK, CHUNK)
        s = _dot(qr_s[rows, :], _lanes2(kr_s[pl.ds(c * QK, QK), :]) * mkr_ref[...]) * dmask_ref[...]
        v = rv_s[rows, :]
        st = st_r[...]
        w = _lanes4(st.astype(BF16)) * mwr_ref[...]
        o_s[rows, VW:2 * VW] = _dot(s.astype(BF16), _tile4(v) * mv_ref[...]) + _dot(qd_s[rows, :], w)
        kt = kd_s[pl.ds(c * QK, QK), :][:, :CHUNK]
        upd = []
        for h in range(HEADS):
            k_h = jnp.concatenate([kt[h * half:(h + 1) * half, :],
                                   kt[LANES + h * half:LANES + (h + 1) * half, :]], axis=0)
            upd.append(_dot(k_h, v[:, h * DV:(h + 1) * DV]))
        upd = [u[:half] for u in upd] + [u[half:] for u in upd]
        st_r[...] = st * rdecay_ref[...] + jnp.concatenate(upd, axis=0)

    def mix_operand():
        gout = gout_ref[...]
        gates = (gr_s, rg_s)
        parts = []
        for h in range(2 * HEADS):
            oh = o_s[:, h * DV:(h + 1) * DV]
            gate = gates[h // HEADS][:, (h % HEADS) * DV:(h % HEADS + 1) * DV]
            parts.append((_rms(oh, gout[:, h * DV:(h + 1) * DV]) * _silu(gate)).astype(BF16))
        return jnp.concatenate(parts, axis=1)

    mixer_units = []
    for c in range(tile // CHUNK):
        mixer_units += [functools.partial(gla_chunk, c), functools.partial(ret_chunk, c)]

    if flush:
        for unit in mixer_units:
            unit()
        out_ref[0] = h1_s[...] + _dot(mix_operand(), wout_ref[...])
        return

    mixer_units[0]()
    x = x_ref[0]
    xn = _rms(x, g1_ref[...]).astype(BF16)
    pieces = _ffn_pieces(wd_ref.shape[0])
    n_pieces = len(pieces)
    spread = max(n_pieces - 1, 1)
    n_rest = len(mixer_units) - 1
    f = None
    mix = None
    gu = _swiglu_up(xn, wg_ref, wu_ref, pieces[0])
    for j in range(n_pieces):
        gu_next = _swiglu_up(xn, wg_ref, wu_ref, pieces[j + 1]) if j + 1 < n_pieces else None
        fj = _swiglu_down(gu, wd_ref, pieces[j])
        f = fj if f is None else f + fj
        gu = gu_next
        if j < spread:
            for u in range(1 + (j * n_rest) // spread, 1 + ((j + 1) * n_rest) // spread):
                mixer_units[u]()
        if j == spread - 1:
            mix = mix_operand()
    out_ref[0] = h1_s[...] + _dot(mix, wout_ref[...])
    h1 = x + 0.5 * f
    n = _rms(h1, gmix_ref[...]).astype(BF16)

    glr = _dot(n, wlr_ref[...])
    gqk = _dot(n, wgqk_ref[...])
    rq = _dot(n, wrq_ref[...])
    rk = _dot(n, wrk_ref[...])
    gq = gqk[:, :QK]
    gk = gqk[:, QK:]

    logit = _dot(glr.astype(BF16), wup_ref[...]) + bgate_ref[...]
    log_a = _log_sigmoid(logit) * (1.0 / GATE_NORM)
    la_hi = log_a.astype(BF16)
    la_lo = (log_a - la_hi.astype(F32)).astype(BF16)
    ll = ll_ref[...]
    cums, cls = [], []
    for r in range(tile // CUM_ROWS):
        blk = slice(r * CUM_ROWS, (r + 1) * CUM_ROWS)
        cc2 = _dot(ll, jnp.concatenate([la_hi[blk], la_lo[blk]], axis=1))
        cc = cc2[:, :QK] + cc2[:, QK:]
        cums.append(cc[:CUM_ROWS])
        cls.append(cc[CUM_ROWS:])
    cum = jnp.concatenate(cums, axis=0)
    cl = jnp.concatenate(cls, axis=0)
    e_pos = jnp.exp(cum)
    e_neg = jnp.exp(-cum)
    qs = gq * (DK ** -0.5)

    cos = cos_ref[...]
    sin = sin_ref[...]

    def rope(t):
        t1 = t[:, :LANES]
        t2 = t[:, LANES:]
        return jnp.concatenate([t1 * cos - t2 * sin, t2 * cos + t1 * sin], axis=1)

    qr = rope(rq) * (DK ** -0.5)
    kr = rope(rk)

    h1_s[...] = h1
    qf_s[...] = (qs * e_pos).astype(BF16)
    qb_s[...] = (qs * e_neg).astype(BF16)
    _store_chunks_transposed(kf_s, gk * e_neg)
    _store_chunks_transposed(kb_s, gk * e_pos)
    _store_chunks_transposed(ks_s, gk * jnp.exp(cl - cum))
    _store_chunks_transposed(dec_s, jnp.exp(cl))
    qr_s[...] = qr.astype(BF16)
    qd_s[...] = (qr * qdec_ref[...]).astype(BF16)
    _store_chunks_transposed(kr_s, kr)
    _store_chunks_transposed(kd_s, kr * kdec_ref[...])

    gvr = _dot(n, wgvr_ref[...])
    gv_s[...] = gvr[:, :VW].astype(BF16)
    gr_s[...] = gvr[:, VW:]
    rvg = _dot(n, wrvg_ref[...])
    rv_s[...] = rvg[:, :VW].astype(BF16)
    rg_s[...] = rvg[:, VW:]


def _call_b_kernel(h_ref, p_ref, g2_ref, wg_ref, wu_ref, wd_ref, gple_ref, wpg_ref, wpp_ref, gfin_ref,
                   out_ref, *, final):
    h = h_ref[0]
    xn = _rms(h, g2_ref[...]).astype(BF16)
    pieces = _ffn_pieces(wd_ref.shape[0])
    f = None
    gu = _swiglu_up(xn, wg_ref, wu_ref, pieces[0])
    for j, cols in enumerate(pieces):
        gu_next = _swiglu_up(xn, wg_ref, wu_ref, pieces[j + 1]) if j + 1 < len(pieces) else None
        fj = _swiglu_down(gu, wd_ref, cols)
        f = fj if f is None else f + fj
        gu = gu_next
    pe = _dot(p_ref[0].astype(BF16), wpp_ref[...])
    h3 = h + 0.5 * f
    n3 = _rms(h3, gple_ref[...]).astype(BF16)
    group_rows = h.shape[0] // TAIL_GROUPS_B
    groups = [slice(k * group_rows, (k + 1) * group_rows) for k in range(TAIL_GROUPS_B)]
    gates = [_dot(n3[rows], wpg_ref[...]) for rows in groups]
    for rows, gate in zip(groups, gates):
        h4 = h3[rows] + _sigmoid(gate) * pe[rows]
        out_ref[0, rows, :] = _rms(h4, gfin_ref[...]) if final else h4


def _const_spec(shape):
    return pl.BlockSpec(shape, lambda *_: (0,) * len(shape), pipeline_mode=pl.Buffered(1))


def _mixer_constants(tile):
    r = np.arange(QK)
    kcol_g = r // DK
    kcol_r = (r % LANES) // (DK // 2)
    row_h = r // CHUNK
    mkg = (row_h[:, None] == kcol_g[None, :])
    mkr = (row_h[:, None] == kcol_r[None, :])
    mv = (row_h[:, None] == (np.arange(VW) // DV)[None, :])
    vrow_h = np.arange(VW) // DV
    mwg = (vrow_h[:, None] == kcol_g[None, :])
    mwr = (vrow_h[:, None] == kcol_r[None, :])
    i = np.arange(CHUNK)
    j = r % CHUNK
    causal = (i[:, None] >= j[None, :]).astype(np.float32)
    t = np.arange(CUM_ROWS)
    ll = np.concatenate([
        (t[:, None] // CHUNK == t[None, :] // CHUNK) & (t[None, :] <= t[:, None]),
        (t[:, None] // CHUNK == t[None, :] // CHUNK)], axis=0)

    log_gamma = np.log(1.0 - 2.0 ** (-5.0 - np.arange(HEADS, dtype=np.float64)))
    pos = np.arange(CHUNK, dtype=np.float64)
    dist = np.abs(pos[:, None] - pos[None, :])
    dmask = np.exp(log_gamma[:, None, None] * dist)
    dmask = dmask.transpose(1, 0, 2).reshape(CHUNK, QK)
    lg_col = log_gamma[kcol_r]
    qdec = np.exp(lg_col[None, :] * (pos + 1.0)[:, None])
    kdec = np.exp(lg_col[None, :] * (CHUNK - 1 - pos)[:, None])
    rdecay = np.tile(np.exp(lg_col * CHUNK)[:, None], (1, DV))
    reps = tile // CHUNK
    as_bf = lambda m: jnp.asarray(m.astype(np.float32), dtype=BF16)
    as_f32 = lambda m: jnp.asarray(m.astype(np.float32))
    return dict(ll=as_bf(ll), mkg=as_bf(mkg.T), mkr=as_bf(mkr.T), mv=as_bf(mv), mwg=as_bf(mwg.T), mwr=as_bf(mwr.T),
                causal=as_f32(causal), dmask=as_f32(dmask), qdec=as_f32(np.tile(qdec, (reps, 1))),
                kdec=as_f32(np.tile(kdec, (reps, 1))), rdecay=as_f32(rdecay))


def _rope_tables(seq):
    half = DK // 2
    inv = ROPE_BASE ** (-np.arange(half, dtype=np.float64) / half)
    ang = np.arange(seq, dtype=np.float64)[:, None] * inv[None, :]
    table = lambda t: jnp.asarray(np.tile(t, (1, HEADS)).astype(np.float32))
    return table(np.cos(ang)), table(np.sin(ang))


def _rope_halves_first(w):
    rows = w.shape[0]
    return w.reshape(rows, HEADS, 2, DK // 2).transpose(0, 2, 1, 3).reshape(rows, QK)


def _call_a_with_casts_kernel(*refs, n_in, n_cast, **kw):
    ins, cast_in = refs[:n_in], refs[n_in:n_in + n_cast]
    out_ref, cast_out = refs[n_in + n_cast], refs[n_in + n_cast + 1:n_in + 2 * n_cast + 1]
    scratch = refs[n_in + 2 * n_cast + 1:]
    for src, dst in zip(cast_in, cast_out):
        dst[...] = src[...].astype(BF16)
    step, last = pl.program_id(0), pl.num_programs(0) - 1
    pl.when(step < last)(functools.partial(_call_a_kernel, *ins, out_ref, *scratch, flush=False, **kw))
    pl.when(step == last)(functools.partial(_call_a_kernel, *ins, out_ref, *scratch, flush=True, **kw))


def _cast_block_rows(rows, max_steps):
    for block in range(BF16_SUBLANES, rows + 1, BF16_SUBLANES):
        if rows % block == 0 and rows // block <= max_steps:
            return block
    raise ValueError(f"no row block for {rows} rows in {max_steps} steps")


def _call_a(h, cos, sin, consts, g1, wg, wu, wd, gmix, win_parts, wup, bgate, gout, wout, cast_weights):
    b, s, d = h.shape
    tile = TILE_A
    tps = s // tile
    n_tiles = b * tps
    row = lambda a: a.reshape(1, -1)
    args = [h, cos, sin, row(g1), wg, wu, wd, row(gmix), *win_parts, wup, row(bgate), row(gout), wout,
            consts["ll"], consts["mkg"], consts["mkr"], consts["mv"], consts["mwg"], consts["mwr"],
            consts["causal"], consts["dmask"], consts["qdec"], consts["kdec"], consts["rdecay"]]
    proj_tile = lambda st: jnp.minimum(st, n_tiles - 1)
    mix_tile = lambda st: jnp.maximum(st - 1, 0)
    in_specs = [pl.BlockSpec((1, tile, d), lambda st: (proj_tile(st) // tps, proj_tile(st) % tps, 0)),
                pl.BlockSpec((tile, LANES), lambda st: (proj_tile(st) % tps, 0)),
                pl.BlockSpec((tile, LANES), lambda st: (proj_tile(st) % tps, 0))]
    in_specs += [_const_spec(a.shape) for a in args[3:]]
    bf = lambda *shape: pltpu.VMEM(shape, BF16)
    f32 = lambda *shape: pltpu.VMEM(shape, F32)
    kt = bf(tile // CHUNK * QK, 2 * CHUNK)
    scratch = [f32(QK, DV), f32(QK, DV), f32(tile, 2 * VW), f32(tile, d), f32(tile, VW), f32(tile, VW),
               bf(tile, QK), bf(tile, QK), kt, kt, kt, bf(tile, VW),
               f32(tile // CHUNK * QK, 2 * CHUNK),
               bf(tile, QK), bf(tile, QK), kt, kt, bf(tile, VW)]
    out_specs = [pl.BlockSpec((1, tile, d), lambda st: (mix_tile(st) // tps, mix_tile(st) % tps, 0))]
    out_shape = [jax.ShapeDtypeStruct((b, s, d), F32)]
    n_in = len(args)
    for w in cast_weights:
        rows, cols = w.shape
        block = _cast_block_rows(rows, n_tiles)
        spec = pl.BlockSpec((block, cols), functools.partial(lambda st, last: (jnp.minimum(st, last), 0),
                                                             last=rows // block - 1))
        in_specs.append(spec)
        out_specs.append(spec)
        out_shape.append(jax.ShapeDtypeStruct(w.shape, BF16))
    outs = pl.pallas_call(
        functools.partial(_call_a_with_casts_kernel, n_in=n_in, n_cast=len(cast_weights),
                          tile=tile, tiles_per_seq=tps),
        grid=(n_tiles + 1,),
        in_specs=in_specs,
        out_specs=out_specs,
        out_shape=out_shape,
        scratch_shapes=scratch,
        compiler_params=pltpu.CompilerParams(dimension_semantics=("arbitrary",),
                                             vmem_limit_bytes=VMEM_LIMIT),
        name="ffn_a_mixers",
    )(*args, *cast_weights)
    return outs[0], outs[1:]


def _call_b(h, p, g2, wg, wu, wd, gple, wpg, wpp, gfin, final):
    b, s, d = h.shape
    tile = TILE_B
    row = lambda a: a.reshape(1, -1)
    args = [h, p, row(g2), wg, wu, wd, row(gple), wpg, wpp, row(gfin)]
    in_specs = [pl.BlockSpec((1, tile, d), lambda bi, si: (bi, si, 0)),
                pl.BlockSpec((1, tile, p.shape[-1]), lambda bi, si: (bi, si, 0))]
    in_specs += [_const_spec(a.shape) for a in args[2:]]
    return pl.pallas_call(
        functools.partial(_call_b_kernel, final=final),
        grid=(b, s // tile),
        in_specs=in_specs,
        out_specs=pl.BlockSpec((1, tile, d), lambda bi, si: (bi, si, 0)),
        out_shape=jax.ShapeDtypeStruct((b, s, d), F32),
        compiler_params=pltpu.CompilerParams(dimension_semantics=("arbitrary", "arbitrary"),
                                             vmem_limit_bytes=VMEM_LIMIT),
        name="ffn_b_embed",
    )(*args)


def kernel(x, p, g_ffn1, w_ffn1_gate, w_ffn1_up, w_ffn1_down, g_mix, w_in, w_gla_gate_up, b_gla_gate, g_gla_out, g_ret_out, w_out, g_ffn2, w_ffn2_gate, w_ffn2_up, w_ffn2_down, g_ple, w_ple_gate, w_ple_proj, g_final):
    depth = p.shape[0]
    seq = x.shape[1]
    assert seq % TILE_A == 0 and seq % TILE_B == 0 and TILE_A % CHUNK == 0
    assert w_ffn1_gate.shape[2] % MXU_COLS == 0
    consts = _mixer_constants(TILE_A)
    cos, sin = _rope_tables(seq)
    sizes = [QK, QK, VW, VW, GATE_RANK, QK, QK, VW, VW]
    offs = np.concatenate([[0], np.cumsum(sizes)])
    bf = lambda a: a.astype(BF16)
    h = x
    for i in range(depth):
        wi = w_in[i]
        cols = lambda j0, j1: bf(wi[:, offs[j0]:offs[j1]])
        win_parts = [jnp.pad(cols(4, 5), ((0, 0), (0, GATE_RANK_PAD - GATE_RANK))),
                     cols(0, 2),
                     _rope_halves_first(cols(5, 6)), _rope_halves_first(cols(6, 7)),
                     cols(2, 4), cols(7, 9)]
        wup = bf(jnp.pad(w_gla_gate_up[i], ((0, GATE_RANK_PAD - GATE_RANK), (0, 0))))
        gout = jnp.concatenate([g_gla_out[i], g_ret_out[i]])
        next_weights = [w_ffn2_gate[i], w_ffn2_up[i], w_ffn2_down[i], w_ple_gate[i], w_ple_proj[i]]
        h, (wg2, wu2, wd2, wpg, wpp) = _call_a(
            h, cos, sin, consts, g_ffn1[i], bf(w_ffn1_gate[i]), bf(w_ffn1_up[i]), bf(w_ffn1_down[i]),
            g_mix[i], win_parts, wup, b_gla_gate[i], gout, bf(w_out[i]), next_weights)
        h = _call_b(h, p[i], g_ffn2[i], wg2, wu2, wd2, g_ple[i], wpg, wpp, g_final, final=(i == depth - 1))
    return h
```
